```python
import math
import jax, jax.numpy as jnp
from jax import lax
import numpy as np

D_MODEL = 2048
BATCH = 2
SEQ = 8192
DEPTH = 4

GRID_W = 64
HEAD_DIM = 128
N_MIX_HEADS = D_MODEL // HEAD_DIM
MIX_WIDTH = N_MIX_HEADS * HEAD_DIM
N_FOURIER_GROUPS = 4
N_POOL_GROUPS = 4
POOL_WINDOWS = (2, 4, 8, 16)
GROUP_W = MIX_WIDTH // (N_FOURIER_GROUPS + N_POOL_GROUPS)
N_ATT_HEADS = N_MIX_HEADS // 2
N_KV_HEADS = 2
N_NA_HEADS = N_MIX_HEADS // 2
NA_WIN_ROWS = 8
NA_WIN_COLS = 16
Q_BLOCK = 128
ROPE_THETA = 10000.0
ATT_SPLITS = (N_ATT_HEADS * HEAD_DIM, N_KV_HEADS * HEAD_DIM, N_KV_HEADS * HEAD_DIM,
              N_NA_HEADS * HEAD_DIM, N_NA_HEADS * HEAD_DIM, N_NA_HEADS * HEAD_DIM)
ATT_IN_WIDTH = sum(ATT_SPLITS)
ATT_OUT_WIDTH = (N_ATT_HEADS + N_NA_HEADS) * HEAD_DIM
PEER_HEADS = 8
PEER_N_KEYS = 128
PEER_N_EXPERTS = PEER_N_KEYS ** 2
PEER_TOPK = 16
PEER_QUERY_DIM = 256
PEER_BLOCK = 128
DEEPNORM_ALPHA = (2.0 * DEPTH) ** 0.25
DEEPNORM_BETA = (8.0 * DEPTH) ** -0.25
LN_EPS = 1e-6
N_EVEN = (DEPTH + 1) // 2
N_ODD = DEPTH // 2

kernel_name = "hybrid_fourier_pool_gqa_natten_peer_encoder"


def _layer_norm(x):
    xf = x.astype(jnp.float32)
    mu = jnp.mean(xf, axis=-1, keepdims=True)
    var = jnp.mean(jnp.square(xf - mu), axis=-1, keepdims=True)
    return (xf - mu) * lax.rsqrt(var + LN_EPS)


def _post_norm(x, g, b):
    return (_layer_norm(x) * g.astype(jnp.float32) + b.astype(jnp.float32)).astype(x.dtype)


def _rms_norm(x, g):
    xf = x.astype(jnp.float32)
    y = xf * lax.rsqrt(jnp.mean(jnp.square(xf), axis=-1, keepdims=True) + LN_EPS)
    return (y * g.astype(jnp.float32)).astype(x.dtype)


def _adaln(c, w, b):
    m = c @ w + b
    shift, scale, gate = jnp.split(m, 3, axis=-1)
    return shift[:, None, :], scale[:, None, :], (1.0 + gate)[:, None, :]


def _modulate(x, shift, scale):
    return (_layer_norm(x) * (1.0 + scale.astype(jnp.float32)) + shift.astype(jnp.float32)).astype(x.dtype)


def _fourier_mix(z, w_fourier):
    zf = jnp.fft.fftn(z.astype(jnp.float32), axes=(1, 3), norm="ortho").real
    return jnp.einsum('bsgc,gcd->bsgd', zf.astype(z.dtype), w_fourier)


def _pool_mix(z, w_pool, pool_scale):
    B, S, G, Cg = z.shape
    zf = z.astype(jnp.float32)
    cs = jnp.concatenate([jnp.zeros((B, 1, G, Cg), jnp.float32), jnp.cumsum(zf, axis=1)], axis=1)
    t = jnp.arange(S)
    outs = []
    for g, w in enumerate(POOL_WINDOWS):
        lo = jnp.clip(t - w // 2, 0, S)
        hi = jnp.clip(t + w // 2, 0, S)
        csg = cs[:, :, g]
        sums = jnp.take(csg, hi, axis=1) - jnp.take(csg, lo, axis=1)
        cnt = (hi - lo).astype(jnp.float32)[None, :, None]
        outs.append(sums / cnt - zf[:, :, g])
    p = jnp.stack(outs, axis=2).astype(z.dtype)
    y = jnp.einsum('bsgc,gcd->bsgd', p, w_pool)
    return y * pool_scale.reshape(G, Cg)


def _fourier_pool_mixer(h, w_in, w_fourier, w_pool, pool_scale, w_out):
    B, S, _ = h.shape
    z = (h @ w_in).reshape(B, S, N_FOURIER_GROUPS + N_POOL_GROUPS, GROUP_W)
    ya = _fourier_mix(z[:, :, :N_FOURIER_GROUPS], w_fourier)
    yb = _pool_mix(z[:, :, N_FOURIER_GROUPS:], w_pool, pool_scale)
    y = jnp.concatenate([ya, yb], axis=2).reshape(B, S, MIX_WIDTH)
    return y @ w_out


def _rope_half(x, pos):
    half = x.shape[-1] // 2
    inv_freq = ROPE_THETA ** (-jnp.arange(half, dtype=jnp.float32) / half)
    ang = pos.astype(jnp.float32)[:, None] * inv_freq[None, :]
    cos = jnp.cos(ang)[None, :, None, :]
    sin = jnp.sin(ang)[None, :, None, :]
    xf = x.astype(jnp.float32)
    x1, x2 = xf[..., :half], xf[..., half:]
    return jnp.concatenate([x1 * cos - x2 * sin, x2 * cos + x1 * sin], axis=-1).astype(x.dtype)


def _axial_rope(x):
    S = x.shape[1]
    t = jnp.arange(S)
    d = x.shape[-1] // 2
    return jnp.concatenate([_rope_half(x[..., :d], t // GRID_W), _rope_half(x[..., d:], t % GRID_W)], axis=-1)


def _gqa_attention(q, k, v):
    B, S, Hq, Dh = q.shape
    Hkv = k.shape[2]
    G = Hq // Hkv
    nb = S // Q_BLOCK
    qb = q.reshape(B, nb, Q_BLOCK, Hkv, G, Dh).transpose(1, 0, 2, 3, 4, 5)
    scale = Dh ** -0.5

    def block(qi):
        s = jnp.einsum('bqkgd,bskd->bkgqs', qi, k).astype(jnp.float32) * scale
        p = jax.nn.softmax(s, axis=-1).astype(v.dtype)
        return jnp.einsum('bkgqs,bskd->bqkgd', p, v)

    o = lax.map(block, qb)
    return o.transpose(1, 0, 2, 3, 4, 5).reshape(B, S, Hq * Dh)


def _neighbourhood_attention(q, k, v, rpb):
    B, S, H, Dh = q.shape
    rows = S // GRID_W
    wr = min(NA_WIN_ROWS, rows)
    wc = NA_WIN_COLS
    qg = q.reshape(B, rows, GRID_W, H, Dh)
    kg = k.reshape(B, rows, GRID_W, H, Dh)
    vg = v.reshape(B, rows, GRID_W, H, Dh)
    cols = np.arange(GRID_W)
    col_start = np.clip(cols - wc // 2, 0, GRID_W - wc)
    col_idx = col_start[:, None] + np.arange(wc)[None, :]
    col_off = col_idx - cols[:, None] + (NA_WIN_COLS - 1)
    bias_cols = rpb[:, :, col_off]
    scale = Dh ** -0.5

    def row_fn(r):
        rs = jnp.clip(r - wr // 2, 0, rows - wr)
        kr = lax.dynamic_slice_in_dim(kg, rs, wr, axis=1)[:, :, col_idx]
        vr = lax.dynamic_slice_in_dim(vg, rs, wr, axis=1)[:, :, col_idx]
        qr = lax.dynamic_index_in_dim(qg, r, axis=1, keepdims=False)
        s = jnp.einsum('bjhd,bijwhd->bhjiw', qr, kr).astype(jnp.float32) * scale
        row_off = rs + jnp.arange(wr) - r + (NA_WIN_ROWS - 1)
        bias = jnp.take(bias_cols, row_off, axis=1)
        s = s + bias.transpose(0, 2, 1, 3)[None].astype(jnp.float32)
        p = jax.nn.softmax(s.reshape(B, H, GRID_W, wr * wc), axis=-1).reshape(s.shape).astype(v.dtype)
        return jnp.einsum('bhjiw,bijwhd->bjhd', p, vr)

    o = lax.map(row_fn, jnp.arange(rows))
    return o.transpose(1, 0, 2, 3, 4).reshape(B, S, H * Dh)


def _attention_mixer(h, w_in, q_norm, k_norm, rpb, w_out):
    B, S, _ = h.shape
    proj = h @ w_in
    splits = np.cumsum(ATT_SPLITS)[:-1].tolist()
    qc, kc, vc, qd, kd, vd = jnp.split(proj, splits, axis=-1)
    qc = qc.reshape(B, S, N_ATT_HEADS, HEAD_DIM)
    kc = kc.reshape(B, S, N_KV_HEADS, HEAD_DIM)
    vc = vc.reshape(B, S, N_KV_HEADS, HEAD_DIM)
    qc = _axial_rope(_rms_norm(qc, q_norm))
    kc = _axial_rope(_rms_norm(kc, k_norm))
    yc = _gqa_attention(qc, kc, vc)
    qd = qd.reshape(B, S, N_NA_HEADS, HEAD_DIM)
    kd = kd.reshape(B, S, N_NA_HEADS, HEAD_DIM)
    vd = vd.reshape(B, S, N_NA_HEADS, HEAD_DIM)
    yd = _neighbourhood_attention(qd, kd, vd, rpb)
    return jnp.concatenate([yc, yd], axis=-1) @ w_out


def _peer(h, w_q, sub_keys, u, v):
    B, S, D = h.shape
    K = PEER_TOPK
    dh = PEER_QUERY_DIM // 2
    q = (h @ w_q).reshape(B, S, PEER_HEADS, 2, dh)
    s = jnp.einsum('bshpd,hpnd->bshpn', q, sub_keys).astype(jnp.float32)
    s1, i1 = lax.top_k(s[..., 0, :], K)
    s2, i2 = lax.top_k(s[..., 1, :], K)
    cand = (s1[..., :, None] + s2[..., None, :]).reshape(B, S, PEER_HEADS, K * K)
    sc, ci = lax.top_k(cand, K)
    e1 = jnp.take_along_axis(i1, ci // K, axis=-1)
    e2 = jnp.take_along_axis(i2, ci % K, axis=-1)
    experts = (e1 * PEER_N_KEYS + e2).reshape(B, S, PEER_HEADS * K)
    gates = jax.nn.softmax(sc, axis=-1).reshape(B, S, PEER_HEADS * K).astype(h.dtype)
    nb = S // PEER_BLOCK

    def to_blocks(a):
        return a.reshape(B, nb, PEER_BLOCK, *a.shape[2:]).swapaxes(0, 1)

    def block(args):
        hb, eb, gb = args
        ub = jnp.take(u, eb, axis=0)
        a = jax.nn.gelu(jnp.einsum('bqd,bqed->bqe', hb, ub), approximate=False)
        vb = jnp.take(v, eb, axis=0)
        return jnp.einsum('bqe,bqed->bqd', gb * a, vb)

    y = lax.map(block, (to_blocks(h), to_blocks(experts), to_blocks(gates)))
    return y.swapaxes(0, 1).reshape(B, S, D)


def setup_inputs(seed: int = 0) -> dict:
    key = jax.random.key(seed)
    ks = jax.random.split(key, 20)
    D = D_MODEL
    f32 = jnp.float32
    nrm = lambda k, shape, s: jax.random.normal(k, shape, f32) * s
    x = nrm(ks[0], (BATCH, SEQ, D), 1.0)
    c = nrm(ks[1], (BATCH, D), 1.0)
    ada_w = nrm(ks[2], (DEPTH, 2, D, 3 * D), 0.1 * D ** -0.5)
    ada_b = nrm(ks[3], (DEPTH, 2, 3 * D), 0.01)
    ln_g = 1.0 + nrm(ks[4], (DEPTH, 2, D), 0.02)
    ln_b = nrm(ks[5], (DEPTH, 2, D), 0.02)
    fp_w_in = nrm(ks[6], (N_EVEN, D, MIX_WIDTH), D ** -0.5)
    fp_w_fourier = nrm(ks[7], (N_EVEN, N_FOURIER_GROUPS, GROUP_W, GROUP_W), GROUP_W ** -0.5)
    fp_w_pool = nrm(ks[8], (N_EVEN, N_POOL_GROUPS, GROUP_W, GROUP_W), GROUP_W ** -0.5)
    fp_pool_scale = 1.0 + nrm(ks[9], (N_EVEN, N_POOL_GROUPS * GROUP_W), 0.1)
    fp_w_out = nrm(ks[10], (N_EVEN, MIX_WIDTH, D), DEEPNORM_BETA * MIX_WIDTH ** -0.5)
    col_scale = jnp.concatenate([
        jnp.full((w,), DEEPNORM_BETA if i in (2, 5) else 1.0, f32) for i, w in enumerate(ATT_SPLITS)])
    at_w_in = nrm(ks[11], (N_ODD, D, ATT_IN_WIDTH), D ** -0.5) * col_scale
    at_q_norm = 1.0 + nrm(ks[12], (N_ODD, HEAD_DIM), 0.02)
    at_k_norm = 1.0 + nrm(ks[13], (N_ODD, HEAD_DIM), 0.02)
    at_rpb = nrm(ks[14], (N_ODD, N_NA_HEADS, 2 * NA_WIN_ROWS - 1, 2 * NA_WIN_COLS - 1), 0.1)
    at_w_out = nrm(ks[15], (N_ODD, ATT_OUT_WIDTH, D), DEEPNORM_BETA * ATT_OUT_WIDTH ** -0.5)
    peer_w_q = nrm(ks[16], (DEPTH, D, PEER_HEADS * PEER_QUERY_DIM), D ** -0.5)
    peer_sub_keys = nrm(ks[17], (DEPTH, PEER_HEADS, 2, PEER_N_KEYS, PEER_QUERY_DIM // 2),
                        (PEER_QUERY_DIM // 2) ** -0.5)
    peer_u = nrm(ks[18], (DEPTH, PEER_N_EXPERTS, D), D ** -0.5)
    peer_v = nrm(ks[19], (DEPTH, PEER_N_EXPERTS, D), DEEPNORM_BETA * PEER_HEADS ** -0.5)
    return {"x": x, "c": c, "ada_w": ada_w, "ada_b": ada_b, "ln_g": ln_g, "ln_b": ln_b,
            "fp_w_in": fp_w_in, "fp_w_fourier": fp_w_fourier, "fp_w_pool": fp_w_pool,
            "fp_pool_scale": fp_pool_scale, "fp_w_out": fp_w_out,
            "at_w_in": at_w_in, "at_q_norm": at_q_norm, "at_k_norm": at_k_norm,
            "at_rpb": at_rpb, "at_w_out": at_w_out,
            "peer_w_q": peer_w_q, "peer_sub_keys": peer_sub_keys, "peer_u": peer_u, "peer_v": peer_v}


def reference(x, c, ada_w, ada_b, ln_g, ln_b, fp_w_in, fp_w_fourier, fp_w_pool, fp_pool_scale,
              fp_w_out, at_w_in, at_q_norm, at_k_norm, at_rpb, at_w_out,
              peer_w_q, peer_sub_keys, peer_u, peer_v):
    for l in range(DEPTH):
        i = l // 2
        shift, scale, gate = _adaln(c, ada_w[l, 0], ada_b[l, 0])
        h = _modulate(x, shift, scale)
        if l % 2 == 0:
            y = _fourier_pool_mixer(h, fp_w_in[i], fp_w_fourier[i], fp_w_pool[i], fp_pool_scale[i], fp_w_out[i])
        else:
            y = _attention_mixer(h, at_w_in[i], at_q_norm[i], at_k_norm[i], at_rpb[i], at_w_out[i])
        x = _post_norm(DEEPNORM_ALPHA * x + gate * y, ln_g[l, 0], ln_b[l, 0])
        shift, scale, gate = _adaln(c, ada_w[l, 1], ada_b[l, 1])
        h = _modulate(x, shift, scale)
        y = _peer(h, peer_w_q[l], peer_sub_keys[l], peer_u[l], peer_v[l])
        x = _post_norm(DEEPNORM_ALPHA * x + gate * y, ln_g[l, 1], ln_b[l, 1])
    return x
```

```python
import functools
import math

import numpy as np
import jax
import jax.numpy as jnp
from jax import lax
from jax.experimental import pallas as pl
from jax.experimental.pallas import tpu as pltpu

F32 = jnp.float32
BF16 = jnp.bfloat16

GRID_COLS = 64
HEAD = 128
GROUP_W = 256
N_FOURIER = 4
N_POOL = 4
POOL_WINDOWS = (2, 4, 8, 16)
N_Q_HEADS = 8
N_KV_HEADS = 2
N_NA_HEADS = 8
NA_ROWS = 8
NA_COLS = 16
ROPE_THETA = 10000.0
PEER_HEADS = 8
PEER_KEYS = 128
PEER_TOPK = 16
LN_EPS = 1e-6
FFT_N1 = 128
FFT_K1_BLOCK = 8

VMEM_LIMIT_BYTES = 56 * 1024 * 1024
LANES = 128
NEG_BIG = -1e30


def _params(*sem):
    return pltpu.CompilerParams(dimension_semantics=sem, vmem_limit_bytes=VMEM_LIMIT_BYTES)


def _dot(a, b):
    return jnp.dot(a, b, preferred_element_type=F32)


def _dot_nt(a, b):
    return lax.dot_general(a, b, (((1,), (1,)), ((), ())), preferred_element_type=F32)


def _layer_norm(x):
    mu = jnp.mean(x, axis=-1, keepdims=True)
    xc = x - mu
    var = jnp.mean(xc * xc, axis=-1, keepdims=True)
    return xc * lax.rsqrt(var + LN_EPS)


def _adaln_kernel(c_ref, w_ref, b_ref, o_ref):
    c = c_ref[...].astype(BF16)
    w = w_ref[0].astype(BF16)
    o_ref[0] = _dot(c, w) + b_ref[0]


def _adaln_all(c, ada_w, ada_b):
    depth, two, d, d3 = ada_w.shape
    n = depth * two
    b = c.shape[0]
    c_pad = jnp.zeros((8, d), F32).at[:b].set(c)
    w = ada_w.reshape(n, d, d3)
    bias = ada_b.reshape(n, 1, d3)
    tn = 768
    return pl.pallas_call(
        _adaln_kernel,
        grid=(n, d3 // tn),
        in_specs=[pl.BlockSpec((8, d), lambda i, j: (0, 0)),
                  pl.BlockSpec((1, d, tn), lambda i, j: (i, 0, j)),
                  pl.BlockSpec((1, 1, tn), lambda i, j: (i, 0, j))],
        out_specs=pl.BlockSpec((1, 8, tn), lambda i, j: (i, 0, j)),
        out_shape=jax.ShapeDtypeStruct((n, 8, d3), F32),
        compiler_params=_params("parallel", "parallel"),
        name="adaln",
    )(c_pad, w, bias)


def _modulate_kernel(x_ref, sh_ref, sc_ref, o_ref):
    y = _layer_norm(x_ref[...])
    o_ref[...] = (y * (1.0 + sc_ref[0]) + sh_ref[0]).astype(o_ref.dtype)


def _modulate(x, shift, scale, seq):
    t, d = x.shape
    tm = 512
    per_b = seq // tm
    return pl.pallas_call(
        _modulate_kernel,
        grid=(t // tm,),
        in_specs=[pl.BlockSpec((tm, d), lambda i: (i, 0)),
                  pl.BlockSpec((1, 1, d), lambda i: (i // per_b, 0, 0)),
                  pl.BlockSpec((1, 1, d), lambda i: (i // per_b, 0, 0))],
        out_specs=pl.BlockSpec((tm, d), lambda i: (i, 0)),
        out_shape=jax.ShapeDtypeStruct((t, d), BF16),
        compiler_params=_params("parallel"),
        name="modulate",
    )(x, shift, scale)


def _postnorm_kernel(x_ref, y_ref, gate_ref, g_ref, b_ref, o_ref, *, alpha):
    u = alpha * x_ref[...] + (1.0 + gate_ref[0]) * y_ref[...]
    o_ref[...] = _layer_norm(u) * g_ref[...] + b_ref[...]


def _postnorm(x, y, gate, g, b, seq, alpha):
    t, d = x.shape
    tm = 512
    per_b = seq // tm
    return pl.pallas_call(
        functools.partial(_postnorm_kernel, alpha=alpha),
        grid=(t // tm,),
        in_specs=[pl.BlockSpec((tm, d), lambda i: (i, 0)),
                  pl.BlockSpec((tm, d), lambda i: (i, 0)),
                  pl.BlockSpec((1, 1, d), lambda i: (i // per_b, 0, 0)),
                  pl.BlockSpec((1, d), lambda i: (0, 0)),
                  pl.BlockSpec((1, d), lambda i: (0, 0))],
        out_specs=pl.BlockSpec((tm, d), lambda i: (i, 0)),
        out_shape=jax.ShapeDtypeStruct((t, d), F32),
        compiler_params=_params("parallel"),
        name="postnorm",
    )(x, y, gate, g.reshape(1, d), b.reshape(1, d))


def _mm_kernel(a_ref, b_ref, o_ref):
    o_ref[...] = _dot(a_ref[...], b_ref[...]).astype(o_ref.dtype)


def _mm(a, w, out_dtype, tm=512, tn=1024):
    m, k = a.shape
    n = w.shape[1]
    tm = min(tm, m)
    tn = min(tn, n)
    return pl.pallas_call(
        _mm_kernel,
        grid=(n // tn, m // tm),
        in_specs=[pl.BlockSpec((tm, k), lambda j, i: (i, 0)),
                  pl.BlockSpec((k, tn), lambda j, i: (0, j))],
        out_specs=pl.BlockSpec((tm, tn), lambda j, i: (i, j)),
        out_shape=jax.ShapeDtypeStruct((m, n), out_dtype),
        compiler_params=_params("parallel", "parallel"),
        name="mm",
    )(a, w)


def _mm2_kernel(a1_ref, a2_ref, w1_ref, w2_ref, o_ref):
    acc = _dot(a1_ref[...].astype(BF16), w1_ref[...])
    acc = acc + _dot(a2_ref[...].astype(BF16), w2_ref[...])
    o_ref[...] = acc


def _mm2(a1, a2, w, tm=512, tn=1024):
    m, k = a1.shape
    n = w.shape[1]
    tm = min(tm, m)
    tn = min(tn, n)
    return pl.pallas_call(
        _mm2_kernel,
        grid=(n // tn, m // tm),
        in_specs=[pl.BlockSpec((tm, k), lambda j, i: (i, 0)),
                  pl.BlockSpec((tm, k), lambda j, i: (i, 0)),
                  pl.BlockSpec((k, tn), lambda j, i: (0, j)),
                  pl.BlockSpec((k, tn), lambda j, i: (1, j))],
        out_specs=pl.BlockSpec((tm, tn), lambda j, i: (i, j)),
        out_shape=jax.ShapeDtypeStruct((m, n), F32),
        compiler_params=_params("parallel", "parallel"),
        name="mm2",
    )(a1, a2, w, w)


def _fourier_proj_kernel(z_ref, w_ref, a_ref, b_ref):
    r = _dot(z_ref[...], w_ref[0])
    a_ref[...] = r[:, :GROUP_W].astype(a_ref.dtype)
    b_ref[...] = r[:, GROUP_W:].astype(b_ref.dtype)


def _fourier_proj(z, w_ab):
    t = z.shape[0]
    tm = 1024
    wid = N_FOURIER * GROUP_W
    return pl.pallas_call(
        _fourier_proj_kernel,
        grid=(N_FOURIER, t // tm),
        in_specs=[pl.BlockSpec((tm, GROUP_W), lambda g, i: (i, g)),
                  pl.BlockSpec((1, GROUP_W, 2 * GROUP_W), lambda g, i: (g, 0, 0))],
        out_specs=[pl.BlockSpec((tm, GROUP_W), lambda g, i: (i, g)),
                   pl.BlockSpec((tm, GROUP_W), lambda g, i: (i, g))],
        out_shape=[jax.ShapeDtypeStruct((t, wid), BF16), jax.ShapeDtypeStruct((t, wid), BF16)],
        compiler_params=_params("parallel", "parallel"),
        name="fourier_proj",
    )(z, w_ab)


def _fft1_kernel(a_ref, b_ref, c_ref, s_ref, twc_ref, tws_ref, tr_ref, ti_ref):
    a = a_ref[0]
    b = b_ref[0]
    c = c_ref[...]
    s = s_ref[...]
    tr = _dot(c, a) + _dot(s, b)
    ti = _dot(c, b) - _dot(s, a)
    reps = a.shape[1] // LANES
    wc = jnp.concatenate([twc_ref[0]] * reps, axis=1)
    ws = jnp.concatenate([tws_ref[0]] * reps, axis=1)
    tr_ref[0] = (tr * wc + ti * ws).astype(tr_ref.dtype)
    ti_ref[0] = (ti * wc - tr * ws).astype(ti_ref.dtype)


def _fft2_kernel(tr_ref, ti_ref, lr_ref, li_ref, o_ref):
    kb, n2, cols = tr_ref.shape[1:]
    tr = tr_ref[0].reshape(kb * n2, cols)
    ti = ti_ref[0].reshape(kb * n2, cols)
    res = _dot(lr_ref[...], tr) + _dot(li_ref[...], ti)
    o_ref[0] = res.reshape(n2, kb, cols)


def _seq_dft_real(a, b, batch, seq):
    wid = a.shape[1]
    n1 = FFT_N1
    n2 = seq // n1
    kb = FFT_K1_BLOCK
    k = np.arange(n1)
    ang1 = 2.0 * np.pi * np.outer(k, k) / n1
    c1 = jnp.asarray(np.cos(ang1), BF16)
    s1 = jnp.asarray(np.sin(ang1), BF16)
    angt = 2.0 * np.pi * np.outer(np.arange(n2), k) / seq
    twc = jnp.asarray(np.broadcast_to(np.cos(angt)[:, :, None], (n2, n1, LANES)), F32)
    tws = jnp.asarray(np.broadcast_to(np.sin(angt)[:, :, None], (n2, n1, LANES)), F32)
    a3 = a.reshape(batch, n1, n2 * wid)
    b3 = b.reshape(batch, n1, n2 * wid)
    tr, ti = pl.pallas_call(
        _fft1_kernel,
        grid=(batch, n2),
        in_specs=[pl.BlockSpec((1, n1, wid), lambda bb, j: (bb, 0, j)),
                  pl.BlockSpec((1, n1, wid), lambda bb, j: (bb, 0, j)),
                  pl.BlockSpec((n1, n1), lambda bb, j: (0, 0)),
                  pl.BlockSpec((n1, n1), lambda bb, j: (0, 0)),
                  pl.BlockSpec((1, n1, LANES), lambda bb, j: (j, 0, 0)),
                  pl.BlockSpec((1, n1, LANES), lambda bb, j: (j, 0, 0))],
        out_specs=[pl.BlockSpec((1, n1, wid), lambda bb, j: (bb, 0, j)),
                   pl.BlockSpec((1, n1, wid), lambda bb, j: (bb, 0, j))],
        out_shape=[jax.ShapeDtypeStruct((batch, n1, n2 * wid), BF16)] * 2,
        compiler_params=_params("parallel", "parallel"),
        name="fft_stage1",
    )(a3, b3, c1, s1, twc, tws)
    k2 = np.arange(n2)
    ang2 = 2.0 * np.pi * np.outer(k2, k2) / n2
    eye = np.eye(kb)
    lr = np.einsum('kn,jJ->kjJn', np.cos(ang2), eye).reshape(n2 * kb, kb * n2)
    li = np.einsum('kn,jJ->kjJn', np.sin(ang2), eye).reshape(n2 * kb, kb * n2)
    tr4 = tr.reshape(batch, n1, n2, wid)
    ti4 = ti.reshape(batch, n1, n2, wid)
    out = pl.pallas_call(
        _fft2_kernel,
        grid=(batch, n1 // kb),
        in_specs=[pl.BlockSpec((1, kb, n2, wid), lambda bb, j: (bb, j, 0, 0)),
                  pl.BlockSpec((1, kb, n2, wid), lambda bb, j: (bb, j, 0, 0)),
                  pl.BlockSpec((n2 * kb, kb * n2), lambda bb, j: (0, 0)),
                  pl.BlockSpec((n2 * kb, kb * n2), lambda bb, j: (0, 0))],
        out_specs=pl.BlockSpec((1, n2, kb, wid), lambda bb, j: (bb, 0, j, 0)),
        out_shape=jax.ShapeDtypeStruct((batch, n2, n1, wid), F32),
        compiler_params=_params("parallel", "parallel"),
        name="fft_stage2",
    )(tr4, ti4, jnp.asarray(lr, BF16), jnp.asarray(li, BF16))
    return out.reshape(batch * seq, wid)


def _fourier_weights(w_fourier, seq):
    g, cg, _ = w_fourier.shape
    k = np.arange(cg)
    ang = 2.0 * np.pi * np.outer(k, k) / cg
    norm = 1.0 / math.sqrt(seq * cg)
    cs = jnp.asarray(np.concatenate([np.cos(ang), -np.sin(ang)], axis=0) * norm, BF16)
    wf = jnp.transpose(w_fourier, (1, 0, 2)).reshape(cg, g * cg).astype(BF16)
    r = _mm(cs, wf, F32)
    r = r.reshape(2, cg, g, cg)
    return jnp.concatenate([r[0], r[1]], axis=-1).transpose(1, 0, 2).astype(BF16)


POOL_TM = 256
POOL_PAD = 128


def _pool_kernel(z_ref, band_ref, w_ref, ps_ref, o_ref, zp_ref, *, win, seq):
    cg = z_ref.shape[2]
    zp_ref[0:POOL_PAD, :] = jnp.zeros((POOL_PAD, cg), BF16)
    zp_ref[seq + POOL_PAD:seq + 2 * POOL_PAD, :] = jnp.zeros((POOL_PAD, cg), BF16)
    zp_ref[POOL_PAD:seq + POOL_PAD, :] = z_ref[0]
    band = band_ref[...]
    w = w_ref[0]
    ps = ps_ref[0]
    half = win // 2

    def body(i, carry):
        t0 = pl.multiple_of(i * POOL_TM, POOL_TM)
        slab = zp_ref[pl.ds(t0, POOL_TM + 2 * POOL_PAD), :]
        sums = _dot(band, slab)
        t = t0 + lax.broadcasted_iota(jnp.int32, (POOL_TM, cg), 0)
        cnt = (jnp.minimum(t + half, seq) - jnp.maximum(t - half, 0)).astype(F32)
        zc = zp_ref[pl.ds(t0 + POOL_PAD, POOL_TM), :].astype(F32)
        p = sums / cnt - zc
        y = _dot(p.astype(BF16), w) * ps
        o_ref[0, pl.ds(t0, POOL_TM), :] = y.astype(o_ref.dtype)
        return carry

    lax.fori_loop(0, seq // POOL_TM, body, 0)


def _pool_group(z3, w_pool, pool_scale, g, batch, seq):
    win = POOL_WINDOWS[g]
    half = win // 2
    tau = np.arange(POOL_TM)[:, None]
    kap = np.arange(POOL_TM + 2 * POOL_PAD)[None, :]
    off = kap - POOL_PAD - tau
    band = jnp.asarray(((off >= -half) & (off < half)).astype(np.float32), BF16)
    col = N_FOURIER + g
    return pl.pallas_call(
        functools.partial(_pool_kernel, win=win, seq=seq),
        grid=(batch,),
        in_specs=[pl.BlockSpec((1, seq, GROUP_W), lambda bb: (bb, 0, col)),
                  pl.BlockSpec((POOL_TM, POOL_TM + 2 * POOL_PAD), lambda bb: (0, 0)),
                  pl.BlockSpec((1, GROUP_W, GROUP_W), lambda bb: (g, 0, 0)),
                  pl.BlockSpec((1, 1, GROUP_W), lambda bb: (g, 0, 0))],
        out_specs=pl.BlockSpec((1, seq, GROUP_W), lambda bb: (bb, 0, 0)),
        out_shape=jax.ShapeDtypeStruct((batch, seq, GROUP_W), BF16),
        scratch_shapes=[pltpu.VMEM((seq + 2 * POOL_PAD, GROUP_W), BF16)],
        compiler_params=_params("parallel"),
        name=f"pool_w{win}",
    )(z3, band, w_pool, pool_scale)


def _fourier_pool_mixer(h, w_in, w_fourier, w_pool, pool_scale, w_out, batch, seq):
    t = h.shape[0]
    z = _mm(h, w_in.astype(BF16), BF16)
    a, b = _fourier_proj(z, _fourier_weights(w_fourier, seq))
    yf = _seq_dft_real(a, b, batch, seq)
    z3 = z.reshape(batch, seq, z.shape[1])
    wp = w_pool.astype(BF16)
    ps = pool_scale.reshape(N_POOL, 1, GROUP_W)
    yp = jnp.concatenate([_pool_group(z3, wp, ps, g, batch, seq) for g in range(N_POOL)], axis=-1)
    return _mm2(yf, yp.reshape(t, N_POOL * GROUP_W), w_out.astype(BF16))


def _qk_prep_kernel(x_ref, g_ref, cos_ref, sin_ref, o_ref):
    x = x_ref[...].astype(F32)
    y = x * lax.rsqrt(jnp.mean(x * x, axis=-1, keepdims=True) + LN_EPS) * g_ref[0]
    lane = lax.broadcasted_iota(jnp.int32, y.shape, 1)
    first = (lane % (HEAD // 2)) < (HEAD // 4)
    partner = jnp.where(first, pltpu.roll(y, HEAD - HEAD // 4, 1), pltpu.roll(y, HEAD // 4, 1))
    o_ref[...] = (y * cos_ref[...] + partner * sin_ref[...]).astype(o_ref.dtype)


def _rope_tables(seq):
    quarter = HEAD // 4
    inv = ROPE_THETA ** (-np.arange(quarter, dtype=np.float64) / quarter)
    t = np.arange(seq)
    ang_r = (t // GRID_COLS)[:, None] * inv[None, :]
    ang_c = (t % GRID_COLS)[:, None] * inv[None, :]
    cos = np.concatenate([np.cos(ang_r), np.cos(ang_r), np.cos(ang_c), np.cos(ang_c)], axis=1)
    sin = np.concatenate([-np.sin(ang_r), np.sin(ang_r), -np.sin(ang_c), np.sin(ang_c)], axis=1)
    return jnp.asarray(cos, F32), jnp.asarray(sin, F32)


def _qk_prep(proj, gains, seq):
    t = proj.shape[0]
    nh = gains.shape[0]
    tm = 1024
    per_b = seq // tm
    cos, sin = _rope_tables(seq)
    return pl.pallas_call(
        _qk_prep_kernel,
        grid=(t // tm, nh),
        in_specs=[pl.BlockSpec((tm, HEAD), lambda i, hh: (i, hh)),
                  pl.BlockSpec((1, 1, HEAD), lambda i, hh: (hh, 0, 0)),
                  pl.BlockSpec((tm, HEAD), lambda i, hh: (i % per_b, 0)),
                  pl.BlockSpec((tm, HEAD), lambda i, hh: (i % per_b, 0))],
        out_specs=pl.BlockSpec((tm, HEAD), lambda i, hh: (i, hh)),
        out_shape=jax.ShapeDtypeStruct((t, nh * HEAD), BF16),
        compiler_params=_params("parallel", "parallel"),
        name="qk_prep",
    )(proj, gains, cos, sin)


FLASH_TQ = 512
FLASH_TK = 512


def _flash_kernel(q_ref, k_ref, v_ref, o_ref):
    seq = k_ref.shape[0]
    tq = q_ref.shape[0]
    for hh in range(N_Q_HEADS // N_KV_HEADS):
        q = q_ref[:, hh * HEAD:(hh + 1) * HEAD]

        def body(c, carry):
            m, l, acc = carry
            c0 = pl.multiple_of(c * FLASH_TK, FLASH_TK)
            k = k_ref[pl.ds(c0, FLASH_TK), :]
            v = v_ref[pl.ds(c0, FLASH_TK), :]
            s = _dot_nt(q, k)
            m_new = jnp.maximum(m, jnp.max(s, axis=-1, keepdims=True))
            p = jnp.exp(s - m_new)
            corr = jnp.exp(m - m_new)
            l = l * corr + jnp.sum(p, axis=-1, keepdims=True)
            acc = acc * corr + _dot(p.astype(BF16), v)
            return m_new, l, acc

        init = (jnp.full((tq, 1), NEG_BIG, F32), jnp.zeros((tq, 1), F32), jnp.zeros((tq, HEAD), F32))
        m, l, acc = lax.fori_loop(0, seq // FLASH_TK, body, init)
        o_ref[:, hh * HEAD:(hh + 1) * HEAD] = (acc / l).astype(o_ref.dtype)


def _gqa_attention(qk, proj, batch, seq):
    t = qk.shape[0]
    per_b = seq // FLASH_TQ
    gw = (N_Q_HEADS // N_KV_HEADS) * HEAD
    k_col = N_Q_HEADS
    v_col = N_Q_HEADS + N_KV_HEADS
    return pl.pallas_call(
        _flash_kernel,
        grid=(batch, N_KV_HEADS, per_b),
        in_specs=[pl.BlockSpec((FLASH_TQ, gw), lambda bb, kv, i: (bb * per_b + i, kv)),
                  pl.BlockSpec((seq, HEAD), lambda bb, kv, i: (bb, k_col + kv)),
                  pl.BlockSpec((seq, HEAD), lambda bb, kv, i: (bb, v_col + kv))],
        out_specs=pl.BlockSpec((FLASH_TQ, gw), lambda bb, kv, i: (bb * per_b + i, kv)),
        out_shape=jax.ShapeDtypeStruct((t, N_Q_HEADS * HEAD), BF16),
        compiler_params=_params("parallel", "parallel", "parallel"),
        name="gqa_flash",
    )(qk, qk, proj)


NA_ROW_BLOCK = 8


def _na_kernel(q_ref, k_ref, v_ref, bias_ref, o_ref, *, rows, scale):
    i = pl.program_id(2)
    span = NA_ROWS * GRID_COLS
    for rr in range(NA_ROW_BLOCK):
        r = i * NA_ROW_BLOCK + rr
        rs = jnp.clip(r - NA_ROWS // 2, 0, rows - NA_ROWS)
        dr0 = rs - r + (NA_ROWS - 1)
        k0 = pl.multiple_of(rs * GRID_COLS, GRID_COLS)
        q = q_ref[rr * GRID_COLS:(rr + 1) * GRID_COLS, :]
        k = k_ref[pl.ds(k0, span), :]
        v = v_ref[pl.ds(k0, span), :]
        s = _dot_nt(q, k) * scale + bias_ref[0, dr0]
        m = jnp.max(s, axis=-1, keepdims=True)
        p = jnp.exp(s - m)
        p = p / jnp.sum(p, axis=-1, keepdims=True)
        o_ref[rr * GRID_COLS:(rr + 1) * GRID_COLS, :] = _dot(p.astype(BF16), v).astype(o_ref.dtype)


def _na_bias_table(rpb):
    cols = np.arange(GRID_COLS)
    start = np.clip(cols - NA_COLS // 2, 0, GRID_COLS - NA_COLS)
    kc = np.arange(GRID_COLS)
    valid = (kc[None, :] >= start[:, None]) & (kc[None, :] < start[:, None] + NA_COLS)
    off = np.clip(kc[None, :] - cols[:, None] + (NA_COLS - 1), 0, 2 * NA_COLS - 2)
    tab = rpb[:, :, off]
    tab = jnp.where(jnp.asarray(valid)[None, None], tab, NEG_BIG)
    per_dr0 = [jnp.concatenate([tab[:, d + j] for j in range(NA_ROWS)], axis=-1) for d in range(NA_ROWS)]
    return jnp.stack(per_dr0, axis=1).astype(F32)


def _neighbourhood_attention(proj, rpb, batch, seq):
    t = proj.shape[0]
    rows = seq // GRID_COLS
    qb = NA_ROW_BLOCK * GRID_COLS
    per_b = seq // qb
    q_col = (N_Q_HEADS + 2 * N_KV_HEADS)
    k_col = q_col + N_NA_HEADS
    v_col = k_col + N_NA_HEADS
    bias = _na_bias_table(rpb)
    return pl.pallas_call(
        functools.partial(_na_kernel, rows=rows, scale=HEAD ** -0.5),
        grid=(batch, N_NA_HEADS, per_b),
        in_specs=[pl.BlockSpec((qb, HEAD), lambda bb, hh, i: (bb * per_b + i, q_col + hh)),
                  pl.BlockSpec((seq, HEAD), lambda bb, hh, i: (bb, k_col + hh)),
                  pl.BlockSpec((seq, HEAD), lambda bb, hh, i: (bb, v_col + hh)),
                  pl.BlockSpec((1, NA_ROWS, GRID_COLS, NA_ROWS * GRID_COLS), lambda bb, hh, i: (hh, 0, 0, 0))],
        out_specs=pl.BlockSpec((qb, HEAD), lambda bb, hh, i: (bb * per_b + i, hh)),
        out_shape=jax.ShapeDtypeStruct((t, N_NA_HEADS * HEAD), BF16),
        compiler_params=_params("parallel", "parallel", "parallel"),
        name="natten",
    )(proj, proj, proj, bias)


def _attention_mixer(h, w_in, q_norm, k_norm, rpb, w_out, batch, seq):
    proj = _mm(h, w_in.astype(BF16), BF16, tn=1152)
    gains = jnp.concatenate([jnp.tile(q_norm[None] * HEAD ** -0.5, (N_Q_HEADS, 1)),
                             jnp.tile(k_norm[None], (N_KV_HEADS, 1))], axis=0)
    qk = _qk_prep(proj, gains.reshape(N_Q_HEADS + N_KV_HEADS, 1, HEAD).astype(F32), seq)
    yc = _gqa_attention(qk, proj, batch, seq)
    yd = _neighbourhood_attention(proj, rpb, batch, seq)
    return _mm2(yc, yd, w_out.astype(BF16))


def _row(v, i, fill):
    rows = lax.broadcasted_iota(jnp.int32, v.shape, 0)
    return jnp.max(jnp.where(rows == i, v, fill), axis=0, keepdims=True)


def _top16(s, ids, big):
    n = s.shape[1]
    slot = lax.broadcasted_iota(jnp.int32, (PEER_TOPK, n), 0)
    vals = jnp.zeros((PEER_TOPK, n), F32)
    idxs = jnp.zeros((PEER_TOPK, n), jnp.int32)
    for it in range(PEER_TOPK):
        m = jnp.max(s, axis=0, keepdims=True)
        ix = jnp.min(jnp.where(s == m, ids, big), axis=0, keepdims=True)
        s = jnp.where(ids == ix, -jnp.inf, s)
        vals = jnp.where(slot == it, m, vals)
        idxs = jnp.where(slot == it, ix, idxs)
    return vals, idxs


def _pick(table, sel):
    out = jnp.zeros_like(table)
    for r in range(PEER_TOPK):
        out = jnp.where(sel == r, _row(table, r, -1), out)
    return out


def _peer_topk_kernel(q_ref, keys_ref, e1_ref, e2_ref, g_ref):
    tm = q_ref.shape[0]
    half = q_ref.shape[1] // 2
    k = PEER_TOPK
    s1 = _dot_nt(keys_ref[0, 0], q_ref[:, :half])
    s2 = _dot_nt(keys_ref[0, 1], q_ref[:, half:])
    key_ids = lax.broadcasted_iota(jnp.int32, (PEER_KEYS, tm), 0)
    v1, i1 = _top16(s1, key_ids, PEER_KEYS)
    v2, i2 = _top16(s2, key_ids, PEER_KEYS)
    i8 = lax.broadcasted_iota(jnp.int32, (8, tm), 0)
    i16 = lax.broadcasted_iota(jnp.int32, (k, tm), 0)
    ninf = -jnp.inf
    cand = [_row(v1, 0, ninf) + v2]
    ids = [i16]
    for i in (1, 2, 3):
        cand.append(_row(v1, i, ninf) + v2[0:8])
        ids.append(i * k + i8)
    cand.append(v1[8:16] + _row(v2, 0, ninf))
    ids.append((i8 + 8) * k)
    for j in (0, 1):
        cand.append(jnp.where(i8 < 4, ninf, v1[0:8] + _row(v2, j, ninf)))
        ids.append(i8 * k + j)
    cand = jnp.concatenate(cand, axis=0)
    ids = jnp.concatenate(ids, axis=0)
    sc, ci = _top16(cand, ids, k * k)
    e1 = _pick(i1, lax.shift_right_logical(ci, 4))
    e2 = _pick(i2, lax.bitwise_and(ci, k - 1))
    ex = jnp.exp(sc - _row(sc, 0, ninf))
    gates = ex / jnp.sum(ex, axis=0, keepdims=True)
    e1_ref[0] = e1.astype(F32)
    e2_ref[0] = e2.astype(F32)
    g_ref[0] = gates


def _peer_topk(q, sub_keys):
    t = q.shape[0]
    tm = 256
    qd = q.shape[1] // PEER_HEADS
    out = jax.ShapeDtypeStruct((PEER_HEADS, PEER_TOPK, t), F32)
    spec = pl.BlockSpec((1, PEER_TOPK, tm), lambda i, hh: (hh, 0, i))
    return pl.pallas_call(
        _peer_topk_kernel,
        grid=(t // tm, PEER_HEADS),
        in_specs=[pl.BlockSpec((tm, qd), lambda i, hh: (i, hh)),
                  pl.BlockSpec((1, 2, PEER_KEYS, qd // 2), lambda i, hh: (hh, 0, 0, 0))],
        out_specs=[spec, spec, spec],
        out_shape=[out, out, out],
        compiler_params=_params("parallel", "parallel"),
        name="peer_topk",
    )(q, sub_keys)


PEER_GB_TOKENS = 128


def _peer_gates_kernel(e1_ref, e2_ref, g_ref, o_ref, e1s, e2s, gs):
    tb = o_ref.shape[0]
    slots = PEER_HEADS * PEER_TOPK
    e1s[...] = e1_ref[...].reshape(slots, tb).T
    e2s[...] = e2_ref[...].reshape(slots, tb).T
    gs[...] = g_ref[...].reshape(slots, tb).T
    key = lax.broadcasted_iota(jnp.int32, (PEER_KEYS, slots), 0).astype(F32)

    def body(t, carry):
        r1 = e1s[pl.ds(t, 1), :]
        r2 = e2s[pl.ds(t, 1), :]
        rg = gs[pl.ds(t, 1), :]
        a = jnp.where(key == r1, 1.0, 0.0).astype(BF16)
        b = jnp.where(key == r2, rg, 0.0).astype(BF16)
        o_ref[t] = _dot_nt(a, b).astype(o_ref.dtype)
        return carry

    lax.fori_loop(0, tb, body, 0)


def _peer_gate_matrix(e1, e2, gates):
    t = e1.shape[2]
    tb = PEER_GB_TOKENS
    slots = PEER_HEADS * PEER_TOPK
    spec = pl.BlockSpec((PEER_HEADS, PEER_TOPK, tb), lambda i: (0, 0, i))
    g3 = pl.pallas_call(
        _peer_gates_kernel,
        grid=(t // tb,),
        in_specs=[spec, spec, spec],
        out_specs=pl.BlockSpec((tb, PEER_KEYS, PEER_KEYS), lambda i: (i, 0, 0)),
        out_shape=jax.ShapeDtypeStruct((t, PEER_KEYS, PEER_KEYS), BF16),
        scratch_shapes=[pltpu.VMEM((tb, slots), F32)] * 3,
        compiler_params=_params("parallel"),
        name="peer_gates",
    )(e1, e2, gates)
    return g3.reshape(t, PEER_KEYS * PEER_KEYS)


def _peer_dense_kernel(h_ref, u_ref, v_ref, g_ref, o_ref):
    @pl.when(pl.program_id(1) == 0)
    def _():
        o_ref[...] = jnp.zeros_like(o_ref)

    a = _dot_nt(h_ref[...], u_ref[...])
    act = 0.5 * a * (1.0 + lax.erf(a * (1.0 / math.sqrt(2.0))))
    p = (act * g_ref[...].astype(F32)).astype(BF16)
    o_ref[...] += _dot(p, v_ref[...])


def _peer_dense(h, u, v, gmat):
    t, d = h.shape
    n_exp = u.shape[0]
    tm = min(1024, t)
    te = 256
    return pl.pallas_call(
        _peer_dense_kernel,
        grid=(t // tm, n_exp // te),
        in_specs=[pl.BlockSpec((tm, d), lambda i, e: (i, 0)),
                  pl.BlockSpec((te, d), lambda i, e: (e, 0)),
                  pl.BlockSpec((te, d), lambda i, e: (e, 0)),
                  pl.BlockSpec((tm, te), lambda i, e: (i, e))],
        out_specs=pl.BlockSpec((tm, d), lambda i, e: (i, 0)),
        out_shape=jax.ShapeDtypeStruct((t, d), F32),
        compiler_params=_params("parallel", "arbitrary"),
        name="peer_dense",
    )(h, u, v, gmat)


def _peer(h, w_q, sub_keys, u, v):
    q = _mm(h, w_q.astype(BF16), BF16)
    e1, e2, gates = _peer_topk(q, sub_keys.astype(BF16))
    gmat = _peer_gate_matrix(e1, e2, gates)
    return _peer_dense(h, u.astype(BF16), v.astype(BF16), gmat)


def kernel(x, c, ada_w, ada_b, ln_g, ln_b, fp_w_in, fp_w_fourier, fp_w_pool, fp_pool_scale, fp_w_out,
           at_w_in, at_q_norm, at_k_norm, at_rpb, at_w_out, peer_w_q, peer_sub_keys, peer_u, peer_v):
    batch, seq, d = x.shape
    depth = ada_w.shape[0]
    alpha = (2.0 * depth) ** 0.25
    xt = x.reshape(batch * seq, d)
    ada = _adaln_all(c, ada_w, ada_b)[:, :batch]
    shift = ada[:, :, None, :d]
    scale = ada[:, :, None, d:2 * d]
    gate = ada[:, :, None, 2 * d:]
    for l in range(depth):
        i = l // 2
        n = 2 * l
        h = _modulate(xt, shift[n], scale[n], seq)
        if l % 2 == 0:
            y = _fourier_pool_mixer(h, fp_w_in[i], fp_w_fourier[i], fp_w_pool[i], fp_pool_scale[i],
                                    fp_w_out[i], batch, seq)
        else:
            y = _attention_mixer(h, at_w_in[i], at_q_norm[i], at_k_norm[i], at_rpb[i], at_w_out[i],
                                 batch, seq)
        xt = _postnorm(xt, y, gate[n], ln_g[l, 0], ln_b[l, 0], seq, alpha)
        n = 2 * l + 1
        h = _modulate(xt, shift[n], scale[n], seq)
        y = _peer(h, peer_w_q[l], peer_sub_keys[l], peer_u[l], peer_v[l])
        xt = _postnorm(xt, y, gate[n], ln_g[l, 1], ln_b[l, 1], seq, alpha)
    return xt.reshape(batch, seq, d)
```

```python
import functools
import math

import numpy as np
import jax
import jax.numpy as jnp
from jax import lax
from jax.experimental import pallas as pl
from jax.experimental.pallas import tpu as pltpu

F32 = jnp.float32
BF16 = jnp.bfloat16

GRID_COLS = 64
HEAD = 128
GROUP_W = 256
N_FOURIER = 4
N_POOL = 4
POOL_WINDOWS = (2, 4, 8, 16)
N_Q_HEADS = 8
N_KV_HEADS = 2
N_NA_HEADS = 8
NA_ROWS = 8
NA_COLS = 16
ROPE_THETA = 10000.0
PEER_HEADS = 8
PEER_KEYS = 128
PEER_TOPK = 16
LN_EPS = 1e-6
FFT_N1 = 128
FFT_K1_BLOCK = 8

VMEM_LIMIT_BYTES = 56 * 1024 * 1024
LANES = 128
NEG_BIG = -1e30


def _params(*sem):
    return pltpu.CompilerParams(dimension_semantics=sem, vmem_limit_bytes=VMEM_LIMIT_BYTES)


def _dot(a, b):
    return jnp.dot(a, b, preferred_element_type=F32)


def _dot_nt(a, b):
    return lax.dot_general(a, b, (((1,), (1,)), ((), ())), preferred_element_type=F32)


def _layer_norm(x):
    mu = jnp.mean(x, axis=-1, keepdims=True)
    xc = x - mu
    var = jnp.mean(xc * xc, axis=-1, keepdims=True)
    return xc * lax.rsqrt(var + LN_EPS)


def _adaln_kernel(c_ref, w_ref, b_ref, o_ref):
    c = c_ref[...].astype(BF16)
    w = w_ref[0].astype(BF16)
    o_ref[0] = _dot(c, w) + b_ref[0]


def _adaln_all(c, ada_w, ada_b):
    depth, two, d, d3 = ada_w.shape
    n = depth * two
    b = c.shape[0]
    c_pad = jnp.zeros((8, d), F32).at[:b].set(c)
    w = ada_w.reshape(n, d, d3)
    bias = ada_b.reshape(n, 1, d3)
    tn = 768
    return pl.pallas_call(
        _adaln_kernel,
        grid=(n, d3 // tn),
        in_specs=[pl.BlockSpec((8, d), lambda i, j: (0, 0)),
                  pl.BlockSpec((1, d, tn), lambda i, j: (i, 0, j)),
                  pl.BlockSpec((1, 1, tn), lambda i, j: (i, 0, j))],
        out_specs=pl.BlockSpec((1, 8, tn), lambda i, j: (i, 0, j)),
        out_shape=jax.ShapeDtypeStruct((n, 8, d3), F32),
        compiler_params=_params("parallel", "parallel"),
        name="adaln",
    )(c_pad, w, bias)


def _modulate_kernel(x_ref, sh_ref, sc_ref, o_ref):
    y = _layer_norm(x_ref[...])
    o_ref[...] = (y * (1.0 + sc_ref[0]) + sh_ref[0]).astype(o_ref.dtype)


def _modulate(x, shift, scale, seq):
    t, d = x.shape
    tm = 512
    per_b = seq // tm
    return pl.pallas_call(
        _modulate_kernel,
        grid=(t // tm,),
        in_specs=[pl.BlockSpec((tm, d), lambda i: (i, 0)),
                  pl.BlockSpec((1, 1, d), lambda i: (i // per_b, 0, 0)),
                  pl.BlockSpec((1, 1, d), lambda i: (i // per_b, 0, 0))],
        out_specs=pl.BlockSpec((tm, d), lambda i: (i, 0)),
        out_shape=jax.ShapeDtypeStruct((t, d), BF16),
        compiler_params=_params("parallel"),
        name="modulate",
    )(x, shift, scale)


def _postnorm_kernel(x_ref, y_ref, gate_ref, g_ref, b_ref, o_ref, *, alpha):
    u = alpha * x_ref[...] + (1.0 + gate_ref[0]) * y_ref[...]
    o_ref[...] = _layer_norm(u) * g_ref[...] + b_ref[...]


def _postnorm(x, y, gate, g, b, seq, alpha):
    t, d = x.shape
    tm = 512
    per_b = seq // tm
    return pl.pallas_call(
        functools.partial(_postnorm_kernel, alpha=alpha),
        grid=(t // tm,),
        in_specs=[pl.BlockSpec((tm, d), lambda i: (i, 0)),
                  pl.BlockSpec((tm, d), lambda i: (i, 0)),
                  pl.BlockSpec((1, 1, d), lambda i: (i // per_b, 0, 0)),
                  pl.BlockSpec((1, d), lambda i: (0, 0)),
                  pl.BlockSpec((1, d), lambda i: (0, 0))],
        out_specs=pl.BlockSpec((tm, d), lambda i: (i, 0)),
        out_shape=jax.ShapeDtypeStruct((t, d), F32),
        compiler_params=_params("parallel"),
        name="postnorm",
    )(x, y, gate, g.reshape(1, d), b.reshape(1, d))


def _mm_kernel(a_ref, b_ref, o_ref):
    o_ref[...] = _dot(a_ref[...], b_ref[...]).astype(o_ref.dtype)


def _mm(a, w, out_dtype, tm=512, tn=1024):
    m, k = a.shape
    n = w.shape[1]
    tm = min(tm, m)
    tn = min(tn, n)
    return pl.pallas_call(
        _mm_kernel,
        grid=(n // tn, m // tm),
        in_specs=[pl.BlockSpec((tm, k), lambda j, i: (i, 0)),
                  pl.BlockSpec((k, tn), lambda j, i: (0, j))],
        out_specs=pl.BlockSpec((tm, tn), lambda j, i: (i, j)),
        out_shape=jax.ShapeDtypeStruct((m, n), out_dtype),
        compiler_params=_params("parallel", "parallel"),
        name="mm",
    )(a, w)


def _mm2_kernel(a1_ref, a2_ref, w1_ref, w2_ref, o_ref):
    acc = _dot(a1_ref[...].astype(BF16), w1_ref[...])
    acc = acc + _dot(a2_ref[...].astype(BF16), w2_ref[...])
    o_ref[...] = acc


def _mm2(a1, a2, w, tm=512, tn=1024):
    m, k = a1.shape
    n = w.shape[1]
    tm = min(tm, m)
    tn = min(tn, n)
    return pl.pallas_call(
        _mm2_kernel,
        grid=(n // tn, m // tm),
        in_specs=[pl.BlockSpec((tm, k), lambda j, i: (i, 0)),
                  pl.BlockSpec((tm, k), lambda j, i: (i, 0)),
                  pl.BlockSpec((k, tn), lambda j, i: (0, j)),
                  pl.BlockSpec((k, tn), lambda j, i: (1, j))],
        out_specs=pl.BlockSpec((tm, tn), lambda j, i: (i, j)),
        out_shape=jax.ShapeDtypeStruct((m, n), F32),
        compiler_params=_params("parallel", "parallel"),
        name="mm2",
    )(a1, a2, w, w)


def _fourier_proj_kernel(z_ref, w_ref, a_ref, b_ref):
    r = _dot(z_ref[...], w_ref[0])
    a_ref[...] = r[:, :GROUP_W].astype(a_ref.dtype)
    b_ref[...] = r[:, GROUP_W:].astype(b_ref.dtype)


def _fourier_proj(z, w_ab):
    t = z.shape[0]
    tm = 1024
    wid = N_FOURIER * GROUP_W
    return pl.pallas_call(
        _fourier_proj_kernel,
        grid=(N_FOURIER, t // tm),
        in_specs=[pl.BlockSpec((tm, GROUP_W), lambda g, i: (i, g)),
                  pl.BlockSpec((1, GROUP_W, 2 * GROUP_W), lambda g, i: (g, 0, 0))],
        out_specs=[pl.BlockSpec((tm, GROUP_W), lambda g, i: (i, g)),
                   pl.BlockSpec((tm, GROUP_W), lambda g, i: (i, g))],
        out_shape=[jax.ShapeDtypeStruct((t, wid), BF16), jax.ShapeDtypeStruct((t, wid), BF16)],
        compiler_params=_params("parallel", "parallel"),
        name="fourier_proj",
    )(z, w_ab)


def _fft1_kernel(a_ref, b_ref, c_ref, s_ref, twc_ref, tws_ref, tr_ref, ti_ref):
    a = a_ref[0]
    b = b_ref[0]
    c = c_ref[...]
    s = s_ref[...]
    tr = _dot(c, a) + _dot(s, b)
    ti = _dot(c, b) - _dot(s, a)
    reps = a.shape[1] // LANES
    wc = jnp.concatenate([twc_ref[0]] * reps, axis=1)
    ws = jnp.concatenate([tws_ref[0]] * reps, axis=1)
    tr_ref[0] = (tr * wc + ti * ws).astype(tr_ref.dtype)
    ti_ref[0] = (ti * wc - tr * ws).astype(ti_ref.dtype)


def _fft2_kernel(tr_ref, ti_ref, lr_ref, li_ref, o_ref):
    kb, n2, cols = tr_ref.shape[1:]
    tr = tr_ref[0].reshape(kb * n2, cols)
    ti = ti_ref[0].reshape(kb * n2, cols)
    res = _dot(lr_ref[...], tr) + _dot(li_ref[...], ti)
    o_ref[0] = res.reshape(n2, kb, cols)


def _seq_dft_real(a, b, batch, seq):
    wid = a.shape[1]
    n1 = FFT_N1
    n2 = seq // n1
    kb = FFT_K1_BLOCK
    k = np.arange(n1)
    ang1 = 2.0 * np.pi * np.outer(k, k) / n1
    c1 = jnp.asarray(np.cos(ang1), BF16)
    s1 = jnp.asarray(np.sin(ang1), BF16)
    angt = 2.0 * np.pi * np.outer(np.arange(n2), k) / seq
    twc = jnp.asarray(np.broadcast_to(np.cos(angt)[:, :, None], (n2, n1, LANES)), F32)
    tws = jnp.asarray(np.broadcast_to(np.sin(angt)[:, :, None], (n2, n1, LANES)), F32)
    a3 = a.reshape(batch, n1, n2 * wid)
    b3 = b.reshape(batch, n1, n2 * wid)
    tr, ti = pl.pallas_call(
        _fft1_kernel,
        grid=(batch, n2),
        in_specs=[pl.BlockSpec((1, n1, wid), lambda bb, j: (bb, 0, j)),
                  pl.BlockSpec((1, n1, wid), lambda bb, j: (bb, 0, j)),
                  pl.BlockSpec((n1, n1), lambda bb, j: (0, 0)),
                  pl.BlockSpec((n1, n1), lambda bb, j: (0, 0)),
                  pl.BlockSpec((1, n1, LANES), lambda bb, j: (j, 0, 0)),
                  pl.BlockSpec((1, n1, LANES), lambda bb, j: (j, 0, 0))],
        out_specs=[pl.BlockSpec((1, n1, wid), lambda bb, j: (bb, 0, j)),
                   pl.BlockSpec((1, n1, wid), lambda bb, j: (bb, 0, j))],
        out_shape=[jax.ShapeDtypeStruct((batch, n1, n2 * wid), BF16)] * 2,
        compiler_params=_params("parallel", "parallel"),
        name="fft_stage1",
    )(a3, b3, c1, s1, twc, tws)
    k2 = np.arange(n2)
    ang2 = 2.0 * np.pi * np.outer(k2, k2) / n2
    eye = np.eye(kb)
    lr = np.einsum('kn,jJ->kjJn', np.cos(ang2), eye).reshape(n2 * kb, kb * n2)
    li = np.einsum('kn,jJ->kjJn', np.sin(ang2), eye).reshape(n2 * kb, kb * n2)
    tr4 = tr.reshape(batch, n1, n2, wid)
    ti4 = ti.reshape(batch, n1, n2, wid)
    out = pl.pallas_call(
        _fft2_kernel,
        grid=(batch, n1 // kb),
        in_specs=[pl.BlockSpec((1, kb, n2, wid), lambda bb, j: (bb, j, 0, 0)),
                  pl.BlockSpec((1, kb, n2, wid), lambda bb, j: (bb, j, 0, 0)),
                  pl.BlockSpec((n2 * kb, kb * n2), lambda bb, j: (0, 0)),
                  pl.BlockSpec((n2 * kb, kb * n2), lambda bb, j: (0, 0))],
        out_specs=pl.BlockSpec((1, n2, kb, wid), lambda bb, j: (bb, 0, j, 0)),
        out_shape=jax.ShapeDtypeStruct((batch, n2, n1, wid), F32),
        compiler_params=_params("parallel", "parallel"),
        name="fft_stage2",
    )(tr4, ti4, jnp.asarray(lr, BF16), jnp.asarray(li, BF16))
    return out.reshape(batch * seq, wid)


def _fourier_weights(w_fourier, seq):
    g, cg, _ = w_fourier.shape
    k = np.arange(cg)
    ang = 2.0 * np.pi * np.outer(k, k) / cg
    norm = 1.0 / math.sqrt(seq * cg)
    cs = jnp.asarray(np.concatenate([np.cos(ang), -np.sin(ang)], axis=0) * norm, BF16)
    wf = jnp.transpose(w_fourier, (1, 0, 2)).reshape(cg, g * cg).astype(BF16)
    r = _mm(cs, wf, F32)
    r = r.reshape(2, cg, g, cg)
    return jnp.concatenate([r[0], r[1]], axis=-1).transpose(1, 0, 2).astype(BF16)


POOL_TM = 256
POOL_PAD = 128


def _pool_kernel(z_ref, band_ref, w_ref, ps_ref, o_ref, zp_ref, *, win, seq):
    cg = z_ref.shape[2]
    zp_ref[0:POOL_PAD, :] = jnp.zeros((POOL_PAD, cg), BF16)
    zp_ref[seq + POOL_PAD:seq + 2 * POOL_PAD, :] = jnp.zeros((POOL_PAD, cg), BF16)
    zp_ref[POOL_PAD:seq + POOL_PAD, :] = z_ref[0]
    band = band_ref[...]
    w = w_ref[0]
    ps = ps_ref[0]
    half = win // 2

    def body(i, carry):
        t0 = pl.multiple_of(i * POOL_TM, POOL_TM)
        slab = zp_ref[pl.ds(t0, POOL_TM + 2 * POOL_PAD), :]
        sums = _dot(band, slab)
        t = t0 + lax.broadcasted_iota(jnp.int32, (POOL_TM, cg), 0)
        cnt = (jnp.minimum(t + half, seq) - jnp.maximum(t - half, 0)).astype(F32)
        zc = zp_ref[pl.ds(t0 + POOL_PAD, POOL_TM), :].astype(F32)
        p = sums / cnt - zc
        y = _dot(p.astype(BF16), w) * ps
        o_ref[0, pl.ds(t0, POOL_TM), :] = y.astype(o_ref.dtype)
        return carry

    lax.fori_loop(0, seq // POOL_TM, body, 0)


def _pool_group(z3, w_pool, pool_scale, g, batch, seq):
    win = POOL_WINDOWS[g]
    half = win // 2
    tau = np.arange(POOL_TM)[:, None]
    kap = np.arange(POOL_TM + 2 * POOL_PAD)[None, :]
    off = kap - POOL_PAD - tau
    band = jnp.asarray(((off >= -half) & (off < half)).astype(np.float32), BF16)
    col = N_FOURIER + g
    return pl.pallas_call(
        functools.partial(_pool_kernel, win=win, seq=seq),
        grid=(batch,),
        in_specs=[pl.BlockSpec((1, seq, GROUP_W), lambda bb: (bb, 0, col)),
                  pl.BlockSpec((POOL_TM, POOL_TM + 2 * POOL_PAD), lambda bb: (0, 0)),
                  pl.BlockSpec((1, GROUP_W, GROUP_W), lambda bb: (g, 0, 0)),
                  pl.BlockSpec((1, 1, GROUP_W), lambda bb: (g, 0, 0))],
        out_specs=pl.BlockSpec((1, seq, GROUP_W), lambda bb: (bb, 0, 0)),
        out_shape=jax.ShapeDtypeStruct((batch, seq, GROUP_W), BF16),
        scratch_shapes=[pltpu.VMEM((seq + 2 * POOL_PAD, GROUP_W), BF16)],
        compiler_params=_params("parallel"),
        name=f"pool_w{win}",
    )(z3, band, w_pool, pool_scale)


def _fourier_pool_mixer(h, w_in, w_fourier, w_pool, pool_scale, w_out, batch, seq):
    t = h.shape[0]
    z = _mm(h, w_in.astype(BF16), BF16)
    a, b = _fourier_proj(z, _fourier_weights(w_fourier, seq))
    yf = _seq_dft_real(a, b, batch, seq)
    z3 = z.reshape(batch, seq, z.shape[1])
    wp = w_pool.astype(BF16)
    ps = pool_scale.reshape(N_POOL, 1, GROUP_W)
    yp = jnp.concatenate([_pool_group(z3, wp, ps, g, batch, seq) for g in range(N_POOL)], axis=-1)
    return _mm2(yf, yp.reshape(t, N_POOL * GROUP_W), w_out.astype(BF16))


def _qk_prep_kernel(x_ref, g_ref, cos_ref, sin_ref, o_ref):
    x = x_ref[...].astype(F32)
    y = x * lax.rsqrt(jnp.mean(x * x, axis=-1, keepdims=True) + LN_EPS) * g_ref[0]
    lane = lax.broadcasted_iota(jnp.int32, y.shape, 1)
    first = (lane % (HEAD // 2)) < (HEAD // 4)
    partner = jnp.where(first, pltpu.roll(y, HEAD - HEAD // 4, 1), pltpu.roll(y, HEAD // 4, 1))
    o_ref[...] = (y * cos_ref[...] + partner * sin_ref[...]).astype(o_ref.dtype)


def _rope_tables(seq):
    quarter = HEAD // 4
    inv = ROPE_THETA ** (-np.arange(quarter, dtype=np.float64) / quarter)
    t = np.arange(seq)
    ang_r = (t // GRID_COLS)[:, None] * inv[None, :]
    ang_c = (t % GRID_COLS)[:, None] * inv[None, :]
    cos = np.concatenate([np.cos(ang_r), np.cos(ang_r), np.cos(ang_c), np.cos(ang_c)], axis=1)
    sin = np.concatenate([-np.sin(ang_r), np.sin(ang_r), -np.sin(ang_c), np.sin(ang_c)], axis=1)
    return jnp.asarray(cos, F32), jnp.asarray(sin, F32)


def _qk_prep(proj, gains, seq):
    t = proj.shape[0]
    nh = gains.shape[0]
    tm = 1024
    per_b = seq // tm
    cos, sin = _rope_tables(seq)
    return pl.pallas_call(
        _qk_prep_kernel,
        grid=(t // tm, nh),
        in_specs=[pl.BlockSpec((tm, HEAD), lambda i, hh: (i, hh)),
                  pl.BlockSpec((1, 1, HEAD), lambda i, hh: (hh, 0, 0)),
                  pl.BlockSpec((tm, HEAD), lambda i, hh: (i % per_b, 0)),
                  pl.BlockSpec((tm, HEAD), lambda i, hh: (i % per_b, 0))],
        out_specs=pl.BlockSpec((tm, HEAD), lambda i, hh: (i, hh)),
        out_shape=jax.ShapeDtypeStruct((t, nh * HEAD), BF16),
        compiler_params=_params("parallel", "parallel"),
        name="qk_prep",
    )(proj, gains, cos, sin)


FLASH_TQ = 512
FLASH_TK = 512


def _flash_kernel(q_ref, k_ref, v_ref, o_ref):
    seq = k_ref.shape[0]
    tq = q_ref.shape[0]
    for hh in range(N_Q_HEADS // N_KV_HEADS):
        q = q_ref[:, hh * HEAD:(hh + 1) * HEAD]

        def body(c, carry):
            m, l, acc = carry
            c0 = pl.multiple_of(c * FLASH_TK, FLASH_TK)
            k = k_ref[pl.ds(c0, FLASH_TK), :]
            v = v_ref[pl.ds(c0, FLASH_TK), :]
            s = _dot_nt(q, k)
            m_new = jnp.maximum(m, jnp.max(s, axis=-1, keepdims=True))
            p = jnp.exp2(s - m_new)
            corr = jnp.exp2(m - m_new)
            l = l * corr + jnp.sum(p, axis=-1, keepdims=True)
            acc = acc * corr + _dot(p.astype(BF16), v)
            return m_new, l, acc

        init = (jnp.full((tq, 1), NEG_BIG, F32), jnp.zeros((tq, 1), F32), jnp.zeros((tq, HEAD), F32))
        m, l, acc = lax.fori_loop(0, seq // FLASH_TK, body, init)
        o_ref[:, hh * HEAD:(hh + 1) * HEAD] = (acc / l).astype(o_ref.dtype)


def _gqa_attention(qk, proj, batch, seq):
    t = qk.shape[0]
    per_b = seq // FLASH_TQ
    gw = (N_Q_HEADS // N_KV_HEADS) * HEAD
    k_col = N_Q_HEADS
    v_col = N_Q_HEADS + N_KV_HEADS
    return pl.pallas_call(
        _flash_kernel,
        grid=(batch, N_KV_HEADS, per_b),
        in_specs=[pl.BlockSpec((FLASH_TQ, gw), lambda bb, kv, i: (bb * per_b + i, kv)),
                  pl.BlockSpec((seq, HEAD), lambda bb, kv, i: (bb, k_col + kv)),
                  pl.BlockSpec((seq, HEAD), lambda bb, kv, i: (bb, v_col + kv))],
        out_specs=pl.BlockSpec((FLASH_TQ, gw), lambda bb, kv, i: (bb * per_b + i, kv)),
        out_shape=jax.ShapeDtypeStruct((t, N_Q_HEADS * HEAD), BF16),
        compiler_params=_params("parallel", "parallel", "parallel"),
        name="gqa_flash",
    )(qk, qk, proj)


NA_ROW_BLOCK = 8


def _na_kernel(q_ref, k_ref, v_ref, bias_ref, o_ref, *, rows, scale):
    i = pl.program_id(2)
    span = NA_ROWS * GRID_COLS
    for rr in range(NA_ROW_BLOCK):
        r = i * NA_ROW_BLOCK + rr
        rs = jnp.clip(r - NA_ROWS // 2, 0, rows - NA_ROWS)
        dr0 = rs - r + (NA_ROWS - 1)
        k0 = pl.multiple_of(rs * GRID_COLS, GRID_COLS)
        q = q_ref[rr * GRID_COLS:(rr + 1) * GRID_COLS, :]
        k = k_ref[pl.ds(k0, span), :]
        v = v_ref[pl.ds(k0, span), :]
        s = _dot_nt(q, k) * scale + bias_ref[0, dr0]
        m = jnp.max(s, axis=-1, keepdims=True)
        p = jnp.exp(s - m)
        p = p / jnp.sum(p, axis=-1, keepdims=True)
        o_ref[rr * GRID_COLS:(rr + 1) * GRID_COLS, :] = _dot(p.astype(BF16), v).astype(o_ref.dtype)


def _na_bias_table(rpb):
    cols = np.arange(GRID_COLS)
    start = np.clip(cols - NA_COLS // 2, 0, GRID_COLS - NA_COLS)
    kc = np.arange(GRID_COLS)
    valid = (kc[None, :] >= start[:, None]) & (kc[None, :] < start[:, None] + NA_COLS)
    off = np.clip(kc[None, :] - cols[:, None] + (NA_COLS - 1), 0, 2 * NA_COLS - 2)
    tab = rpb[:, :, off]
    tab = jnp.where(jnp.asarray(valid)[None, None], tab, NEG_BIG)
    per_dr0 = [jnp.concatenate([tab[:, d + j] for j in range(NA_ROWS)], axis=-1) for d in range(NA_ROWS)]
    return jnp.stack(per_dr0, axis=1).astype(F32)


def _neighbourhood_attention(proj, rpb, batch, seq):
    t = proj.shape[0]
    rows = seq // GRID_COLS
    qb = NA_ROW_BLOCK * GRID_COLS
    per_b = seq // qb
    q_col = (N_Q_HEADS + 2 * N_KV_HEADS)
    k_col = q_col + N_NA_HEADS
    v_col = k_col + N_NA_HEADS
    bias = _na_bias_table(rpb)
    return pl.pallas_call(
        functools.partial(_na_kernel, rows=rows, scale=HEAD ** -0.5),
        grid=(batch, N_NA_HEADS, per_b),
        in_specs=[pl.BlockSpec((qb, HEAD), lambda bb, hh, i: (bb * per_b + i, q_col + hh)),
                  pl.BlockSpec((seq, HEAD), lambda bb, hh, i: (bb, k_col + hh)),
                  pl.BlockSpec((seq, HEAD), lambda bb, hh, i: (bb, v_col + hh)),
                  pl.BlockSpec((1, NA_ROWS, GRID_COLS, NA_ROWS * GRID_COLS), lambda bb, hh, i: (hh, 0, 0, 0))],
        out_specs=pl.BlockSpec((qb, HEAD), lambda bb, hh, i: (bb * per_b + i, hh)),
        out_shape=jax.ShapeDtypeStruct((t, N_NA_HEADS * HEAD), BF16),
        compiler_params=_params("parallel", "parallel", "parallel"),
        name="natten",
    )(proj, proj, proj, bias)


def _attention_mixer(h, w_in, q_norm, k_norm, rpb, w_out, batch, seq):
    proj = _mm(h, w_in.astype(BF16), BF16, tn=1152)
    q_gain = HEAD ** -0.5 * math.log2(math.e)
    gains = jnp.concatenate([jnp.tile(q_norm[None] * q_gain, (N_Q_HEADS, 1)),
                             jnp.tile(k_norm[None], (N_KV_HEADS, 1))], axis=0)
    qk = _qk_prep(proj, gains.reshape(N_Q_HEADS + N_KV_HEADS, 1, HEAD).astype(F32), seq)
    yc = _gqa_attention(qk, proj, batch, seq)
    yd = _neighbourhood_attention(proj, rpb, batch, seq)
    return _mm2(yc, yd, w_out.astype(BF16))


def _row(v, i, fill):
    rows = lax.broadcasted_iota(jnp.int32, v.shape, 0)
    return jnp.max(jnp.where(rows == i, v, fill), axis=0, keepdims=True)


def _top16(s, ids, big):
    n = s.shape[1]
    slot = lax.broadcasted_iota(jnp.int32, (PEER_TOPK, n), 0)
    vals = jnp.zeros((PEER_TOPK, n), F32)
    idxs = jnp.zeros((PEER_TOPK, n), jnp.int32)
    for it in range(PEER_TOPK):
        m = jnp.max(s, axis=0, keepdims=True)
        ix = jnp.min(jnp.where(s == m, ids, big), axis=0, keepdims=True)
        s = jnp.where(ids == ix, -jnp.inf, s)
        vals = jnp.where(slot == it, m, vals)
        idxs = jnp.where(slot == it, ix, idxs)
    return vals, idxs


def _pick(table, sel):
    out = jnp.zeros_like(table)
    for r in range(PEER_TOPK):
        out = jnp.where(sel == r, _row(table, r, -1), out)
    return out


def _peer_topk_kernel(q_ref, keys_ref, e1_ref, e2_ref, g_ref):
    tm = q_ref.shape[0]
    half = q_ref.shape[1] // 2
    k = PEER_TOPK
    s1 = _dot_nt(keys_ref[0, 0], q_ref[:, :half])
    s2 = _dot_nt(keys_ref[0, 1], q_ref[:, half:])
    key_ids = lax.broadcasted_iota(jnp.int32, (PEER_KEYS, tm), 0)
    v1, i1 = _top16(s1, key_ids, PEER_KEYS)
    v2, i2 = _top16(s2, key_ids, PEER_KEYS)
    i8 = lax.broadcasted_iota(jnp.int32, (8, tm), 0)
    i16 = lax.broadcasted_iota(jnp.int32, (k, tm), 0)
    ninf = -jnp.inf
    cand = [_row(v1, 0, ninf) + v2]
    ids = [i16]
    for i in (1, 2, 3):
        cand.append(_row(v1, i, ninf) + v2[0:8])
        ids.append(i * k + i8)
    cand.append(v1[8:16] + _row(v2, 0, ninf))
    ids.append((i8 + 8) * k)
    for j in (0, 1):
        cand.append(jnp.where(i8 < 4, ninf, v1[0:8] + _row(v2, j, ninf)))
        ids.append(i8 * k + j)
    cand = jnp.concatenate(cand, axis=0)
    ids = jnp.concatenate(ids, axis=0)
    sc, ci = _top16(cand, ids, k * k)
    e1 = _pick(i1, lax.shift_right_logical(ci, 4))
    e2 = _pick(i2, lax.bitwise_and(ci, k - 1))
    ex = jnp.exp(sc - _row(sc, 0, ninf))
    gates = ex / jnp.sum(ex, axis=0, keepdims=True)
    e1_ref[0] = e1.astype(F32)
    e2_ref[0] = e2.astype(F32)
    g_ref[0] = gates


def _peer_topk(q, sub_keys):
    t = q.shape[0]
    tm = 256
    qd = q.shape[1] // PEER_HEADS
    out = jax.ShapeDtypeStruct((PEER_HEADS, PEER_TOPK, t), F32)
    spec = pl.BlockSpec((1, PEER_TOPK, tm), lambda i, hh: (hh, 0, i))
    return pl.pallas_call(
        _peer_topk_kernel,
        grid=(t // tm, PEER_HEADS),
        in_specs=[pl.BlockSpec((tm, qd), lambda i, hh: (i, hh)),
                  pl.BlockSpec((1, 2, PEER_KEYS, qd // 2), lambda i, hh: (hh, 0, 0, 0))],
        out_specs=[spec, spec, spec],
        out_shape=[out, out, out],
        compiler_params=_params("parallel", "parallel"),
        name="peer_topk",
    )(q, sub_keys)


PEER_TM = 512
PEER_TE = 512
PEER_PASSES = 2
PEER_BUILD_UNROLL = 32


def _peer_experts_kernel(h_ref, e1_ref, e2_ref, gt_ref, u_ref, v_ref, o_ref, gmat, e1s, e2s, gs):
    p = pl.program_id(1)
    c = pl.program_id(2)
    tm = h_ref.shape[0]
    te = u_ref.shape[0]
    slots = PEER_HEADS * PEER_TOPK
    kpp = gmat.shape[0] // tm

    @pl.when((p == 0) & (c == 0))
    def _():
        o_ref[...] = jnp.zeros_like(o_ref)
        e1s[...] = e1_ref[...].reshape(slots, tm).T
        e2s[...] = e2_ref[...].reshape(slots, tm).T
        gs[...] = gt_ref[...].reshape(slots, tm).T

    @pl.when(c == 0)
    def _():
        key1 = (lax.broadcasted_iota(jnp.int32, (kpp, slots), 0) + p * kpp).astype(F32)
        key2 = lax.broadcasted_iota(jnp.int32, (PEER_KEYS, slots), 0).astype(F32)

        def body(t, carry):
            r1 = e1s[pl.ds(t, 1), :]
            r2 = e2s[pl.ds(t, 1), :]
            rg = gs[pl.ds(t, 1), :]
            a = jnp.where(key1 == r1, 1.0, 0.0).astype(BF16)
            b = jnp.where(key2 == r2, rg, 0.0).astype(BF16)
            gmat[pl.ds(pl.multiple_of(t * kpp, kpp), kpp), :] = _dot_nt(a, b)
            return carry

        lax.fori_loop(0, tm, body, 0, unroll=PEER_BUILD_UNROLL)

    a = _dot_nt(h_ref[...], u_ref[...])
    act = 0.5 * a * (1.0 + lax.erf(a * (1.0 / math.sqrt(2.0))))
    per_step = te // PEER_KEYS
    g = jnp.concatenate([gmat[pl.ds(c * per_step + j, tm, stride=kpp), :] for j in range(per_step)], axis=1)
    o_ref[...] += _dot((act * g).astype(BF16), v_ref[...])


def _peer_experts(h, e1, e2, gates, u, v):
    t, d = h.shape
    n_exp = u.shape[0]
    tm = min(PEER_TM, t)
    te = PEER_TE
    kpp = PEER_KEYS // PEER_PASSES
    cpp = n_exp // PEER_PASSES // te
    slots = PEER_HEADS * PEER_TOPK
    sel = pl.BlockSpec((PEER_HEADS, PEER_TOPK, tm), lambda i, p, c: (0, 0, i))
    return pl.pallas_call(
        _peer_experts_kernel,
        grid=(t // tm, PEER_PASSES, cpp),
        in_specs=[pl.BlockSpec((tm, d), lambda i, p, c: (i, 0)),
                  sel, sel, sel,
                  pl.BlockSpec((te, d), lambda i, p, c: (p * cpp + c, 0)),
                  pl.BlockSpec((te, d), lambda i, p, c: (p * cpp + c, 0))],
        out_specs=pl.BlockSpec((tm, d), lambda i, p, c: (i, 0)),
        out_shape=jax.ShapeDtypeStruct((t, d), F32),
        scratch_shapes=[pltpu.VMEM((tm * kpp, PEER_KEYS), F32)] + [pltpu.VMEM((tm, slots), F32)] * 3,
        compiler_params=_params("parallel", "arbitrary", "arbitrary"),
        name="peer_experts",
    )(h, e1, e2, gates, u, v)


def _peer(h, w_q, sub_keys, u, v):
    q = _mm(h, w_q.astype(BF16), BF16)
    e1, e2, gates = _peer_topk(q, sub_keys.astype(BF16))
    return _peer_experts(h, e1, e2, gates, u.astype(BF16), v.astype(BF16))


def kernel(x, c, ada_w, ada_b, ln_g, ln_b, fp_w_in, fp_w_fourier, fp_w_pool, fp_pool_scale, fp_w_out,
           at_w_in, at_q_norm, at_k_norm, at_rpb, at_w_out, peer_w_q, peer_sub_keys, peer_u, peer_v):
    batch, seq, d = x.shape
    depth = ada_w.shape[0]
    alpha = (2.0 * depth) ** 0.25
    xt = x.reshape(batch * seq, d)
    ada = _adaln_all(c, ada_w, ada_b)[:, :batch]
    shift = ada[:, :, None, :d]
    scale = ada[:, :, None, d:2 * d]
    gate = ada[:, :, None, 2 * d:]
    for l in range(depth):
        i = l // 2
        n = 2 * l
        h = _modulate(xt, shift[n], scale[n], seq)
        if l % 2 == 0:
            y = _fourier_pool_mixer(h, fp_w_in[i], fp_w_fourier[i], fp_w_pool[i], fp_pool_scale[i],
                                    fp_w_out[i], batch, seq)
        else:
            y = _attention_mixer(h, at_w_in[i], at_q_norm[i], at_k_norm[i], at_rpb[i], at_w_out[i],
                                 batch, seq)
        xt = _postnorm(xt, y, gate[n], ln_g[l, 0], ln_b[l, 0], seq, alpha)
        n = 2 * l + 1
        h = _modulate(xt, shift[n], scale[n], seq)
        y = _peer(h, peer_w_q[l], peer_sub_keys[l], peer_u[l], peer_v[l])
        xt = _postnorm(xt, y, gate[n], ln_g[l, 1], ln_b[l, 1], seq, alpha)
    return xt.reshape(batch, seq, d)
```

```python
import functools
import math

import numpy as np
import jax
import jax.numpy as jnp
from jax import lax
from jax.experimental import pallas as pl
from jax.experimental.pallas import tpu as pltpu

F32 = jnp.float32
BF16 = jnp.bfloat16

GRID_COLS = 64
HEAD = 128
GROUP_W = 256
N_FOURIER = 4
N_POOL = 4
POOL_WINDOWS = (2, 4, 8, 16)
N_Q_HEADS = 8
N_KV_HEADS = 2
N_NA_HEADS = 8
NA_ROWS = 8
NA_COLS = 16
ROPE_THETA = 10000.0
PEER_HEADS = 8
PEER_KEYS = 128
PEER_TOPK = 16
LN_EPS = 1e-6
FFT_N1 = 128
FFT_K1_BLOCK = 8

VMEM_LIMIT_BYTES = 56 * 1024 * 1024
LANES = 128
NEG_BIG = -1e30


def _params(*sem):
    return pltpu.CompilerParams(dimension_semantics=sem, vmem_limit_bytes=VMEM_LIMIT_BYTES)


def _dot(a, b):
    return jnp.dot(a, b, preferred_element_type=F32)


def _dot_nt(a, b):
    return lax.dot_general(a, b, (((1,), (1,)), ((), ())), preferred_element_type=F32)


def _layer_norm(x):
    mu = jnp.mean(x, axis=-1, keepdims=True)
    xc = x - mu
    var = jnp.mean(xc * xc, axis=-1, keepdims=True)
    return xc * lax.rsqrt(var + LN_EPS)


def _adaln_kernel(c_ref, w_ref, b_ref, o_ref):
    c = c_ref[...].astype(BF16)
    w = w_ref[0].astype(BF16)
    o_ref[0] = _dot(c, w) + b_ref[0]


def _adaln_all(c, ada_w, ada_b):
    depth, two, d, d3 = ada_w.shape
    n = depth * two
    b = c.shape[0]
    c_pad = jnp.zeros((8, d), F32).at[:b].set(c)
    w = ada_w.reshape(n, d, d3)
    bias = ada_b.reshape(n, 1, d3)
    tn = 768
    return pl.pallas_call(
        _adaln_kernel,
        grid=(n, d3 // tn),
        in_specs=[pl.BlockSpec((8, d), lambda i, j: (0, 0)),
                  pl.BlockSpec((1, d, tn), lambda i, j: (i, 0, j)),
                  pl.BlockSpec((1, 1, tn), lambda i, j: (i, 0, j))],
        out_specs=pl.BlockSpec((1, 8, tn), lambda i, j: (i, 0, j)),
        out_shape=jax.ShapeDtypeStruct((n, 8, d3), F32),
        compiler_params=_params("parallel", "parallel"),
        name="adaln",
    )(c_pad, w, bias)


def _modulate_kernel(x_ref, sh_ref, sc_ref, o_ref):
    y = _layer_norm(x_ref[...])
    o_ref[...] = (y * (1.0 + sc_ref[0]) + sh_ref[0]).astype(o_ref.dtype)


def _modulate(x, shift, scale, seq):
    t, d = x.shape
    tm = 512
    per_b = seq // tm
    return pl.pallas_call(
        _modulate_kernel,
        grid=(t // tm,),
        in_specs=[pl.BlockSpec((tm, d), lambda i: (i, 0)),
                  pl.BlockSpec((1, 1, d), lambda i: (i // per_b, 0, 0)),
                  pl.BlockSpec((1, 1, d), lambda i: (i // per_b, 0, 0))],
        out_specs=pl.BlockSpec((tm, d), lambda i: (i, 0)),
        out_shape=jax.ShapeDtypeStruct((t, d), BF16),
        compiler_params=_params("parallel"),
        name="modulate",
    )(x, shift, scale)


def _postnorm_kernel(x_ref, y_ref, gate_ref, g_ref, b_ref, o_ref, *, alpha):
    u = alpha * x_ref[...] + (1.0 + gate_ref[0]) * y_ref[...]
    o_ref[...] = _layer_norm(u) * g_ref[...] + b_ref[...]


def _postnorm(x, y, gate, g, b, seq, alpha):
    t, d = x.shape
    tm = 512
    per_b = seq // tm
    return pl.pallas_call(
        functools.partial(_postnorm_kernel, alpha=alpha),
        grid=(t // tm,),
        in_specs=[pl.BlockSpec((tm, d), lambda i: (i, 0)),
                  pl.BlockSpec((tm, d), lambda i: (i, 0)),
                  pl.BlockSpec((1, 1, d), lambda i: (i // per_b, 0, 0)),
                  pl.BlockSpec((1, d), lambda i: (0, 0)),
                  pl.BlockSpec((1, d), lambda i: (0, 0))],
        out_specs=pl.BlockSpec((tm, d), lambda i: (i, 0)),
        out_shape=jax.ShapeDtypeStruct((t, d), F32),
        compiler_params=_params("parallel"),
        name="postnorm",
    )(x, y, gate, g.reshape(1, d), b.reshape(1, d))


def _mm_kernel(a_ref, b_ref, o_ref):
    o_ref[...] = _dot(a_ref[...], b_ref[...]).astype(o_ref.dtype)


def _mm(a, w, out_dtype, tm=512, tn=1024):
    m, k = a.shape
    n = w.shape[1]
    tm = min(tm, m)
    tn = min(tn, n)
    return pl.pallas_call(
        _mm_kernel,
        grid=(n // tn, m // tm),
        in_specs=[pl.BlockSpec((tm, k), lambda j, i: (i, 0)),
                  pl.BlockSpec((k, tn), lambda j, i: (0, j))],
        out_specs=pl.BlockSpec((tm, tn), lambda j, i: (i, j)),
        out_shape=jax.ShapeDtypeStruct((m, n), out_dtype),
        compiler_params=_params("parallel", "parallel"),
        name="mm",
    )(a, w)


def _mm2_kernel(a1_ref, a2_ref, w1_ref, w2_ref, o_ref):
    acc = _dot(a1_ref[...].astype(BF16), w1_ref[...])
    acc = acc + _dot(a2_ref[...].astype(BF16), w2_ref[...])
    o_ref[...] = acc


def _mm2(a1, a2, w, tm=512, tn=1024):
    m, k = a1.shape
    n = w.shape[1]
    tm = min(tm, m)
    tn = min(tn, n)
    return pl.pallas_call(
        _mm2_kernel,
        grid=(n // tn, m // tm),
        in_specs=[pl.BlockSpec((tm, k), lambda j, i: (i, 0)),
                  pl.BlockSpec((tm, k), lambda j, i: (i, 0)),
                  pl.BlockSpec((k, tn), lambda j, i: (0, j)),
                  pl.BlockSpec((k, tn), lambda j, i: (1, j))],
        out_specs=pl.BlockSpec((tm, tn), lambda j, i: (i, j)),
        out_shape=jax.ShapeDtypeStruct((m, n), F32),
        compiler_params=_params("parallel", "parallel"),
        name="mm2",
    )(a1, a2, w, w)


def _fourier_proj_kernel(z_ref, w_ref, a_ref, b_ref):
    r = _dot(z_ref[...], w_ref[0])
    a_ref[...] = r[:, :GROUP_W].astype(a_ref.dtype)
    b_ref[...] = r[:, GROUP_W:].astype(b_ref.dtype)


def _fourier_proj(z, w_ab):
    t = z.shape[0]
    tm = 1024
    wid = N_FOURIER * GROUP_W
    return pl.pallas_call(
        _fourier_proj_kernel,
        grid=(N_FOURIER, t // tm),
        in_specs=[pl.BlockSpec((tm, GROUP_W), lambda g, i: (i, g)),
                  pl.BlockSpec((1, GROUP_W, 2 * GROUP_W), lambda g, i: (g, 0, 0))],
        out_specs=[pl.BlockSpec((tm, GROUP_W), lambda g, i: (i, g)),
                   pl.BlockSpec((tm, GROUP_W), lambda g, i: (i, g))],
        out_shape=[jax.ShapeDtypeStruct((t, wid), BF16), jax.ShapeDtypeStruct((t, wid), BF16)],
        compiler_params=_params("parallel", "parallel"),
        name="fourier_proj",
    )(z, w_ab)


def _fft1_kernel(a_ref, b_ref, c_ref, s_ref, twc_ref, tws_ref, tr_ref, ti_ref):
    a = a_ref[0]
    b = b_ref[0]
    c = c_ref[...]
    s = s_ref[...]
    tr = _dot(c, a) + _dot(s, b)
    ti = _dot(c, b) - _dot(s, a)
    reps = a.shape[1] // LANES
    wc = jnp.concatenate([twc_ref[0]] * reps, axis=1)
    ws = jnp.concatenate([tws_ref[0]] * reps, axis=1)
    tr_ref[0] = (tr * wc + ti * ws).astype(tr_ref.dtype)
    ti_ref[0] = (ti * wc - tr * ws).astype(ti_ref.dtype)


def _fft2_kernel(tr_ref, ti_ref, lr_ref, li_ref, o_ref):
    kb, n2, cols = tr_ref.shape[1:]
    tr = tr_ref[0].reshape(kb * n2, cols)
    ti = ti_ref[0].reshape(kb * n2, cols)
    res = _dot(lr_ref[...], tr) + _dot(li_ref[...], ti)
    o_ref[0] = res.reshape(n2, kb, cols)


def _seq_dft_real(a, b, batch, seq):
    wid = a.shape[1]
    n1 = FFT_N1
    n2 = seq // n1
    kb = FFT_K1_BLOCK
    k = np.arange(n1)
    ang1 = 2.0 * np.pi * np.outer(k, k) / n1
    c1 = jnp.asarray(np.cos(ang1), BF16)
    s1 = jnp.asarray(np.sin(ang1), BF16)
    angt = 2.0 * np.pi * np.outer(np.arange(n2), k) / seq
    twc = jnp.asarray(np.broadcast_to(np.cos(angt)[:, :, None], (n2, n1, LANES)), F32)
    tws = jnp.asarray(np.broadcast_to(np.sin(angt)[:, :, None], (n2, n1, LANES)), F32)
    a3 = a.reshape(batch, n1, n2 * wid)
    b3 = b.reshape(batch, n1, n2 * wid)
    tr, ti = pl.pallas_call(
        _fft1_kernel,
        grid=(batch, n2),
        in_specs=[pl.BlockSpec((1, n1, wid), lambda bb, j: (bb, 0, j)),
                  pl.BlockSpec((1, n1, wid), lambda bb, j: (bb, 0, j)),
                  pl.BlockSpec((n1, n1), lambda bb, j: (0, 0)),
                  pl.BlockSpec((n1, n1), lambda bb, j: (0, 0)),
                  pl.BlockSpec((1, n1, LANES), lambda bb, j: (j, 0, 0)),
                  pl.BlockSpec((1, n1, LANES), lambda bb, j: (j, 0, 0))],
        out_specs=[pl.BlockSpec((1, n1, wid), lambda bb, j: (bb, 0, j)),
                   pl.BlockSpec((1, n1, wid), lambda bb, j: (bb, 0, j))],
        out_shape=[jax.ShapeDtypeStruct((batch, n1, n2 * wid), BF16)] * 2,
        compiler_params=_params("parallel", "parallel"),
        name="fft_stage1",
    )(a3, b3, c1, s1, twc, tws)
    k2 = np.arange(n2)
    ang2 = 2.0 * np.pi * np.outer(k2, k2) / n2
    eye = np.eye(kb)
    lr = np.einsum('kn,jJ->kjJn', np.cos(ang2), eye).reshape(n2 * kb, kb * n2)
    li = np.einsum('kn,jJ->kjJn', np.sin(ang2), eye).reshape(n2 * kb, kb * n2)
    tr4 = tr.reshape(batch, n1, n2, wid)
    ti4 = ti.reshape(batch, n1, n2, wid)
    out = pl.pallas_call(
        _fft2_kernel,
        grid=(batch, n1 // kb),
        in_specs=[pl.BlockSpec((1, kb, n2, wid), lambda bb, j: (bb, j, 0, 0)),
                  pl.BlockSpec((1, kb, n2, wid), lambda bb, j: (bb, j, 0, 0)),
                  pl.BlockSpec((n2 * kb, kb * n2), lambda bb, j: (0, 0)),
                  pl.BlockSpec((n2 * kb, kb * n2), lambda bb, j: (0, 0))],
        out_specs=pl.BlockSpec((1, n2, kb, wid), lambda bb, j: (bb, 0, j, 0)),
        out_shape=jax.ShapeDtypeStruct((batch, n2, n1, wid), F32),
        compiler_params=_params("parallel", "parallel"),
        name="fft_stage2",
    )(tr4, ti4, jnp.asarray(lr, BF16), jnp.asarray(li, BF16))
    return out.reshape(batch * seq, wid)


def _fourier_weights(w_fourier, seq):
    g, cg, _ = w_fourier.shape
    k = np.arange(cg)
    ang = 2.0 * np.pi * np.outer(k, k) / cg
    norm = 1.0 / math.sqrt(seq * cg)
    cs = jnp.asarray(np.concatenate([np.cos(ang), -np.sin(ang)], axis=0) * norm, BF16)
    wf = jnp.transpose(w_fourier, (1, 0, 2)).reshape(cg, g * cg).astype(BF16)
    r = _mm(cs, wf, F32)
    r = r.reshape(2, cg, g, cg)
    return jnp.concatenate([r[0], r[1]], axis=-1).transpose(1, 0, 2).astype(BF16)


POOL_TM = 256
POOL_PAD = 128


def _pool_kernel(z_ref, band_ref, w_ref, ps_ref, o_ref, zp_ref, *, win, seq):
    cg = z_ref.shape[2]
    zp_ref[0:POOL_PAD, :] = jnp.zeros((POOL_PAD, cg), BF16)
    zp_ref[seq + POOL_PAD:seq + 2 * POOL_PAD, :] = jnp.zeros((POOL_PAD, cg), BF16)
    zp_ref[POOL_PAD:seq + POOL_PAD, :] = z_ref[0]
    band = band_ref[...]
    w = w_ref[0]
    ps = ps_ref[0]
    half = win // 2

    def body(i, carry):
        t0 = pl.multiple_of(i * POOL_TM, POOL_TM)
        slab = zp_ref[pl.ds(t0, POOL_TM + 2 * POOL_PAD), :]
        sums = _dot(band, slab)
        t = t0 + lax.broadcasted_iota(jnp.int32, (POOL_TM, cg), 0)
        cnt = (jnp.minimum(t + half, seq) - jnp.maximum(t - half, 0)).astype(F32)
        zc = zp_ref[pl.ds(t0 + POOL_PAD, POOL_TM), :].astype(F32)
        p = sums / cnt - zc
        y = _dot(p.astype(BF16), w) * ps
        o_ref[0, pl.ds(t0, POOL_TM), :] = y.astype(o_ref.dtype)
        return carry

    lax.fori_loop(0, seq // POOL_TM, body, 0)


def _pool_group(z3, w_pool, pool_scale, g, batch, seq):
    win = POOL_WINDOWS[g]
    half = win // 2
    tau = np.arange(POOL_TM)[:, None]
    kap = np.arange(POOL_TM + 2 * POOL_PAD)[None, :]
    off = kap - POOL_PAD - tau
    band = jnp.asarray(((off >= -half) & (off < half)).astype(np.float32), BF16)
    col = N_FOURIER + g
    return pl.pallas_call(
        functools.partial(_pool_kernel, win=win, seq=seq),
        grid=(batch,),
        in_specs=[pl.BlockSpec((1, seq, GROUP_W), lambda bb: (bb, 0, col)),
                  pl.BlockSpec((POOL_TM, POOL_TM + 2 * POOL_PAD), lambda bb: (0, 0)),
                  pl.BlockSpec((1, GROUP_W, GROUP_W), lambda bb: (g, 0, 0)),
                  pl.BlockSpec((1, 1, GROUP_W), lambda bb: (g, 0, 0))],
        out_specs=pl.BlockSpec((1, seq, GROUP_W), lambda bb: (bb, 0, 0)),
        out_shape=jax.ShapeDtypeStruct((batch, seq, GROUP_W), BF16),
        scratch_shapes=[pltpu.VMEM((seq + 2 * POOL_PAD, GROUP_W), BF16)],
        compiler_params=_params("parallel"),
        name=f"pool_w{win}",
    )(z3, band, w_pool, pool_scale)


def _fourier_pool_mixer(h, w_in, w_fourier, w_pool, pool_scale, w_out, batch, seq):
    t = h.shape[0]
    z = _mm(h, w_in.astype(BF16), BF16)
    a, b = _fourier_proj(z, _fourier_weights(w_fourier, seq))
    yf = _seq_dft_real(a, b, batch, seq)
    z3 = z.reshape(batch, seq, z.shape[1])
    wp = w_pool.astype(BF16)
    ps = pool_scale.reshape(N_POOL, 1, GROUP_W)
    yp = jnp.concatenate([_pool_group(z3, wp, ps, g, batch, seq) for g in range(N_POOL)], axis=-1)
    return _mm2(yf, yp.reshape(t, N_POOL * GROUP_W), w_out.astype(BF16))


def _qk_prep_kernel(x_ref, g_ref, cos_ref, sin_ref, o_ref):
    x = x_ref[...].astype(F32)
    y = x * lax.rsqrt(jnp.mean(x * x, axis=-1, keepdims=True) + LN_EPS) * g_ref[0]
    lane = lax.broadcasted_iota(jnp.int32, y.shape, 1)
    first = (lane % (HEAD // 2)) < (HEAD // 4)
    partner = jnp.where(first, pltpu.roll(y, HEAD - HEAD // 4, 1), pltpu.roll(y, HEAD // 4, 1))
    o_ref[...] = (y * cos_ref[...] + partner * sin_ref[...]).astype(o_ref.dtype)


def _rope_tables(seq):
    quarter = HEAD // 4
    inv = ROPE_THETA ** (-np.arange(quarter, dtype=np.float64) / quarter)
    t = np.arange(seq)
    ang_r = (t // GRID_COLS)[:, None] * inv[None, :]
    ang_c = (t % GRID_COLS)[:, None] * inv[None, :]
    cos = np.concatenate([np.cos(ang_r), np.cos(ang_r), np.cos(ang_c), np.cos(ang_c)], axis=1)
    sin = np.concatenate([-np.sin(ang_r), np.sin(ang_r), -np.sin(ang_c), np.sin(ang_c)], axis=1)
    return jnp.asarray(cos, F32), jnp.asarray(sin, F32)


def _qk_prep(proj, gains, seq):
    t = proj.shape[0]
    nh = gains.shape[0]
    tm = 1024
    per_b = seq // tm
    cos, sin = _rope_tables(seq)
    return pl.pallas_call(
        _qk_prep_kernel,
        grid=(t // tm, nh),
        in_specs=[pl.BlockSpec((tm, HEAD), lambda i, hh: (i, hh)),
                  pl.BlockSpec((1, 1, HEAD), lambda i, hh: (hh, 0, 0)),
                  pl.BlockSpec((tm, HEAD), lambda i, hh: (i % per_b, 0)),
                  pl.BlockSpec((tm, HEAD), lambda i, hh: (i % per_b, 0))],
        out_specs=pl.BlockSpec((tm, HEAD), lambda i, hh: (i, hh)),
        out_shape=jax.ShapeDtypeStruct((t, nh * HEAD), BF16),
        compiler_params=_params("parallel", "parallel"),
        name="qk_prep",
    )(proj, gains, cos, sin)


FLASH_TQ = 512
FLASH_TK = 512


def _flash_kernel(q_ref, k_ref, v_ref, o_ref):
    seq = k_ref.shape[0]
    tq = q_ref.shape[0]
    for hh in range(N_Q_HEADS // N_KV_HEADS):
        q = q_ref[:, hh * HEAD:(hh + 1) * HEAD]

        def body(c, carry):
            m, l, acc = carry
            c0 = pl.multiple_of(c * FLASH_TK, FLASH_TK)
            k = k_ref[pl.ds(c0, FLASH_TK), :]
            v = v_ref[pl.ds(c0, FLASH_TK), :]
            s = _dot_nt(q, k)
            m_new = jnp.maximum(m, jnp.max(s, axis=-1, keepdims=True))
            p = jnp.exp2(s - m_new)
            corr = jnp.exp2(m - m_new)
            l = l * corr + jnp.sum(p, axis=-1, keepdims=True)
            acc = acc * corr + _dot(p.astype(BF16), v)
            return m_new, l, acc

        init = (jnp.full((tq, 1), NEG_BIG, F32), jnp.zeros((tq, 1), F32), jnp.zeros((tq, HEAD), F32))
        m, l, acc = lax.fori_loop(0, seq // FLASH_TK, body, init)
        o_ref[:, hh * HEAD:(hh + 1) * HEAD] = (acc / l).astype(o_ref.dtype)


def _gqa_attention(qk, proj, batch, seq):
    t = qk.shape[0]
    per_b = seq // FLASH_TQ
    gw = (N_Q_HEADS // N_KV_HEADS) * HEAD
    k_col = N_Q_HEADS
    v_col = N_Q_HEADS + N_KV_HEADS
    return pl.pallas_call(
        _flash_kernel,
        grid=(batch, N_KV_HEADS, per_b),
        in_specs=[pl.BlockSpec((FLASH_TQ, gw), lambda bb, kv, i: (bb * per_b + i, kv)),
                  pl.BlockSpec((seq, HEAD), lambda bb, kv, i: (bb, k_col + kv)),
                  pl.BlockSpec((seq, HEAD), lambda bb, kv, i: (bb, v_col + kv))],
        out_specs=pl.BlockSpec((FLASH_TQ, gw), lambda bb, kv, i: (bb * per_b + i, kv)),
        out_shape=jax.ShapeDtypeStruct((t, N_Q_HEADS * HEAD), BF16),
        compiler_params=_params("parallel", "parallel", "parallel"),
        name="gqa_flash",
    )(qk, qk, proj)


NA_ROW_BLOCK = 8


def _na_kernel(q_ref, k_ref, v_ref, bias_ref, o_ref, *, rows, scale):
    i = pl.program_id(2)
    span = NA_ROWS * GRID_COLS
    for rr in range(NA_ROW_BLOCK):
        r = i * NA_ROW_BLOCK + rr
        rs = jnp.clip(r - NA_ROWS // 2, 0, rows - NA_ROWS)
        dr0 = rs - r + (NA_ROWS - 1)
        k0 = pl.multiple_of(rs * GRID_COLS, GRID_COLS)
        q = q_ref[rr * GRID_COLS:(rr + 1) * GRID_COLS, :]
        k = k_ref[pl.ds(k0, span), :]
        v = v_ref[pl.ds(k0, span), :]
        s = _dot_nt(q, k) * scale + bias_ref[0, dr0]
        m = jnp.max(s, axis=-1, keepdims=True)
        p = jnp.exp(s - m)
        p = p / jnp.sum(p, axis=-1, keepdims=True)
        o_ref[rr * GRID_COLS:(rr + 1) * GRID_COLS, :] = _dot(p.astype(BF16), v).astype(o_ref.dtype)


def _na_bias_table(rpb):
    cols = np.arange(GRID_COLS)
    start = np.clip(cols - NA_COLS // 2, 0, GRID_COLS - NA_COLS)
    kc = np.arange(GRID_COLS)
    valid = (kc[None, :] >= start[:, None]) & (kc[None, :] < start[:, None] + NA_COLS)
    off = np.clip(kc[None, :] - cols[:, None] + (NA_COLS - 1), 0, 2 * NA_COLS - 2)
    tab = rpb[:, :, off]
    tab = jnp.where(jnp.asarray(valid)[None, None], tab, NEG_BIG)
    per_dr0 = [jnp.concatenate([tab[:, d + j] for j in range(NA_ROWS)], axis=-1) for d in range(NA_ROWS)]
    return jnp.stack(per_dr0, axis=1).astype(F32)


def _neighbourhood_attention(proj, rpb, batch, seq):
    t = proj.shape[0]
    rows = seq // GRID_COLS
    qb = NA_ROW_BLOCK * GRID_COLS
    per_b = seq // qb
    q_col = (N_Q_HEADS + 2 * N_KV_HEADS)
    k_col = q_col + N_NA_HEADS
    v_col = k_col + N_NA_HEADS
    bias = _na_bias_table(rpb)
    return pl.pallas_call(
        functools.partial(_na_kernel, rows=rows, scale=HEAD ** -0.5),
        grid=(batch, N_NA_HEADS, per_b),
        in_specs=[pl.BlockSpec((qb, HEAD), lambda bb, hh, i: (bb * per_b + i, q_col + hh)),
                  pl.BlockSpec((seq, HEAD), lambda bb, hh, i: (bb, k_col + hh)),
                  pl.BlockSpec((seq, HEAD), lambda bb, hh, i: (bb, v_col + hh)),
                  pl.BlockSpec((1, NA_ROWS, GRID_COLS, NA_ROWS * GRID_COLS), lambda bb, hh, i: (hh, 0, 0, 0))],
        out_specs=pl.BlockSpec((qb, HEAD), lambda bb, hh, i: (bb * per_b + i, hh)),
        out_shape=jax.ShapeDtypeStruct((t, N_NA_HEADS * HEAD), BF16),
        compiler_params=_params("parallel", "parallel", "parallel"),
        name="natten",
    )(proj, proj, proj, bias)


def _attention_mixer(h, w_in, q_norm, k_norm, rpb, w_out, batch, seq):
    proj = _mm(h, w_in.astype(BF16), BF16, tn=1152)
    q_gain = HEAD ** -0.5 * math.log2(math.e)
    gains = jnp.concatenate([jnp.tile(q_norm[None] * q_gain, (N_Q_HEADS, 1)),
                             jnp.tile(k_norm[None], (N_KV_HEADS, 1))], axis=0)
    qk = _qk_prep(proj, gains.reshape(N_Q_HEADS + N_KV_HEADS, 1, HEAD).astype(F32), seq)
    yc = _gqa_attention(qk, proj, batch, seq)
    yd = _neighbourhood_attention(proj, rpb, batch, seq)
    return _mm2(yc, yd, w_out.astype(BF16))


def _row(v, i, fill):
    rows = lax.broadcasted_iota(jnp.int32, v.shape, 0)
    return jnp.max(jnp.where(rows == i, v, fill), axis=0, keepdims=True)


def _top16(s, ids, big):
    n = s.shape[1]
    slot = lax.broadcasted_iota(jnp.int32, (PEER_TOPK, n), 0)
    vals = jnp.zeros((PEER_TOPK, n), F32)
    idxs = jnp.zeros((PEER_TOPK, n), jnp.int32)
    for it in range(PEER_TOPK):
        m = jnp.max(s, axis=0, keepdims=True)
        ix = jnp.min(jnp.where(s == m, ids, big), axis=0, keepdims=True)
        s = jnp.where(ids == ix, -jnp.inf, s)
        vals = jnp.where(slot == it, m, vals)
        idxs = jnp.where(slot == it, ix, idxs)
    return vals, idxs


def _pick(table, sel):
    out = jnp.zeros_like(table)
    for r in range(PEER_TOPK):
        out = jnp.where(sel == r, _row(table, r, -1), out)
    return out


def _peer_topk_kernel(q_ref, keys_ref, e1_ref, e2_ref, g_ref):
    tm = q_ref.shape[0]
    half = q_ref.shape[1] // 2
    k = PEER_TOPK
    s1 = _dot_nt(keys_ref[0, 0], q_ref[:, :half])
    s2 = _dot_nt(keys_ref[0, 1], q_ref[:, half:])
    key_ids = lax.broadcasted_iota(jnp.int32, (PEER_KEYS, tm), 0)
    v1, i1 = _top16(s1, key_ids, PEER_KEYS)
    v2, i2 = _top16(s2, key_ids, PEER_KEYS)
    i8 = lax.broadcasted_iota(jnp.int32, (8, tm), 0)
    i16 = lax.broadcasted_iota(jnp.int32, (k, tm), 0)
    ninf = -jnp.inf
    cand = [_row(v1, 0, ninf) + v2]
    ids = [i16]
    for i in (1, 2, 3):
        cand.append(_row(v1, i, ninf) + v2[0:8])
        ids.append(i * k + i8)
    cand.append(v1[8:16] + _row(v2, 0, ninf))
    ids.append((i8 + 8) * k)
    for j in (0, 1):
        cand.append(jnp.where(i8 < 4, ninf, v1[0:8] + _row(v2, j, ninf)))
        ids.append(i8 * k + j)
    cand = jnp.concatenate(cand, axis=0)
    ids = jnp.concatenate(ids, axis=0)
    sc, ci = _top16(cand, ids, k * k)
    e1 = _pick(i1, lax.shift_right_logical(ci, 4))
    e2 = _pick(i2, lax.bitwise_and(ci, k - 1))
    ex = jnp.exp(sc - _row(sc, 0, ninf))
    gates = ex / jnp.sum(ex, axis=0, keepdims=True)
    e1_ref[0] = e1.astype(F32)
    e2_ref[0] = e2.astype(F32)
    g_ref[0] = gates


def _peer_topk(q, sub_keys):
    t = q.shape[0]
    tm = 256
    qd = q.shape[1] // PEER_HEADS
    out = jax.ShapeDtypeStruct((PEER_HEADS, PEER_TOPK, t), F32)
    spec = pl.BlockSpec((1, PEER_TOPK, tm), lambda i, hh: (hh, 0, i))
    return pl.pallas_call(
        _peer_topk_kernel,
        grid=(t // tm, PEER_HEADS),
        in_specs=[pl.BlockSpec((tm, qd), lambda i, hh: (i, hh)),
                  pl.BlockSpec((1, 2, PEER_KEYS, qd // 2), lambda i, hh: (hh, 0, 0, 0))],
        out_specs=[spec, spec, spec],
        out_shape=[out, out, out],
        compiler_params=_params("parallel", "parallel"),
        name="peer_topk",
    )(q, sub_keys)


PEER_TM = 512
PEER_TE = 512
PEER_PASSES = 2
PEER_BUILD_UNROLL = 32
PEER_ROW_PAD = 8


def _peer_experts_kernel(h_ref, e1_ref, e2_ref, gt_ref, u_ref, v_ref, o_ref, gmat, e1s, e2s, gs):
    p = pl.program_id(1)
    c = pl.program_id(2)
    tm = h_ref.shape[0]
    te = u_ref.shape[0]
    slots = PEER_HEADS * PEER_TOPK
    pitch = gmat.shape[0] // tm
    kpp = pitch - PEER_ROW_PAD

    @pl.when((p == 0) & (c == 0))
    def _():
        o_ref[...] = jnp.zeros_like(o_ref)
        e1s[...] = e1_ref[...].reshape(slots, tm).T
        e2s[...] = e2_ref[...].reshape(slots, tm).T
        gs[...] = gt_ref[...].reshape(slots, tm).T

    @pl.when(c == 0)
    def _():
        key1 = (lax.broadcasted_iota(jnp.int32, (kpp, slots), 0) + p * kpp).astype(F32)
        key2 = lax.broadcasted_iota(jnp.int32, (PEER_KEYS, slots), 0).astype(F32)

        def body(t, carry):
            r1 = e1s[pl.ds(t, 1), :]
            r2 = e2s[pl.ds(t, 1), :]
            rg = gs[pl.ds(t, 1), :]
            a = jnp.where(key1 == r1, 1.0, 0.0).astype(BF16)
            b = jnp.where(key2 == r2, rg, 0.0).astype(BF16)
            gmat[pl.ds(pl.multiple_of(t * pitch, PEER_ROW_PAD), kpp), :] = _dot_nt(a, b)
            return carry

        lax.fori_loop(0, tm, body, 0, unroll=PEER_BUILD_UNROLL)

    a = _dot_nt(h_ref[...], u_ref[...])
    act = 0.5 * a * (1.0 + lax.erf(a * (1.0 / math.sqrt(2.0))))
    per_step = te // PEER_KEYS
    g = jnp.concatenate([gmat[pl.ds(c * per_step + j, tm, stride=pitch), :] for j in range(per_step)], axis=1)
    o_ref[...] += _dot((act * g).astype(BF16), v_ref[...])


def _peer_experts(h, e1, e2, gates, u, v):
    t, d = h.shape
    n_exp = u.shape[0]
    tm = min(PEER_TM, t)
    te = PEER_TE
    kpp = PEER_KEYS // PEER_PASSES
    cpp = n_exp // PEER_PASSES // te
    slots = PEER_HEADS * PEER_TOPK
    sel = pl.BlockSpec((PEER_HEADS, PEER_TOPK, tm), lambda i, p, c: (0, 0, i))
    return pl.pallas_call(
        _peer_experts_kernel,
        grid=(t // tm, PEER_PASSES, cpp),
        in_specs=[pl.BlockSpec((tm, d), lambda i, p, c: (i, 0)),
                  sel, sel, sel,
                  pl.BlockSpec((te, d), lambda i, p, c: (p * cpp + c, 0)),
                  pl.BlockSpec((te, d), lambda i, p, c: (p * cpp + c, 0))],
        out_specs=pl.BlockSpec((tm, d), lambda i, p, c: (i, 0)),
        out_shape=jax.ShapeDtypeStruct((t, d), F32),
        scratch_shapes=[pltpu.VMEM((tm * (kpp + PEER_ROW_PAD), PEER_KEYS), F32)]
        + [pltpu.VMEM((tm, slots), F32)] * 3,
        compiler_params=_params("parallel", "arbitrary", "arbitrary"),
        name="peer_experts",
    )(h, e1, e2, gates, u, v)


def _peer(h, w_q, sub_keys, u, v):
    q = _mm(h, w_q.astype(BF16), BF16)
    e1, e2, gates = _peer_topk(q, sub_keys.astype(BF16))
    return _peer_experts(h, e1, e2, gates, u.astype(BF16), v.astype(BF16))


def kernel(x, c, ada_w, ada_b, ln_g, ln_b, fp_w_in, fp_w_fourier, fp_w_pool, fp_pool_scale, fp_w_out,
           at_w_in, at_q_norm, at_k_norm, at_rpb, at_w_out, peer_w_q, peer_sub_keys, peer_u, peer_v):
    batch, seq, d = x.shape
    depth = ada_w.shape[0]
    alpha = (2.0 * depth) ** 0.25
    xt = x.reshape(batch * seq, d)
    ada = _adaln_all(c, ada_w, ada_b)[:, :batch]
    shift = ada[:, :, None, :d]
    scale = ada[:, :, None, d:2 * d]
    gate = ada[:, :, None, 2 * d:]
    for l in range(depth):
        i = l // 2
        n = 2 * l
        h = _modulate(xt, shift[n], scale[n], seq)
        if l % 2 == 0:
            y = _fourier_pool_mixer(h, fp_w_in[i], fp_w_fourier[i], fp_w_pool[i], fp_pool_scale[i],
                                    fp_w_out[i], batch, seq)
        else:
            y = _attention_mixer(h, at_w_in[i], at_q_norm[i], at_k_norm[i], at_rpb[i], at_w_out[i],
                                 batch, seq)
        xt = _postnorm(xt, y, gate[n], ln_g[l, 0], ln_b[l, 0], seq, alpha)
        n = 2 * l + 1
        h = _modulate(xt, shift[n], scale[n], seq)
        y = _peer(h, peer_w_q[l], peer_sub_keys[l], peer_u[l], peer_v[l])
        xt = _postnorm(xt, y, gate[n], ln_g[l, 1], ln_b[l, 1], seq, alpha)
    return xt.reshape(batch, seq, d)
```

```python
import functools
import math

import numpy as np
import jax
import jax.numpy as jnp
from jax import lax
from jax.experimental import pallas as pl
from jax.experimental.pallas import tpu as pltpu

F32 = jnp.float32
BF16 = jnp.bfloat16

GRID_COLS = 64
HEAD = 128
GROUP_W = 256
N_FOURIER = 4
N_POOL = 4
POOL_WINDOWS = (2, 4, 8, 16)
N_Q_HEADS = 8
N_KV_HEADS = 2
N_NA_HEADS = 8
NA_ROWS = 8
NA_COLS = 16
ROPE_THETA = 10000.0
PEER_HEADS = 8
PEER_KEYS = 128
PEER_TOPK = 16
LN_EPS = 1e-6
FFT_N1 = 128
FFT_K1_BLOCK = 8

VMEM_LIMIT_BYTES = 56 * 1024 * 1024
LANES = 128
NEG_BIG = -1e30


def _params(*sem):
    return pltpu.CompilerParams(dimension_semantics=sem, vmem_limit_bytes=VMEM_LIMIT_BYTES)


def _dot(a, b):
    return jnp.dot(a, b, preferred_element_type=F32)


def _dot_nt(a, b):
    return lax.dot_general(a, b, (((1,), (1,)), ((), ())), preferred_element_type=F32)


def _layer_norm(x):
    mu = jnp.mean(x, axis=-1, keepdims=True)
    xc = x - mu
    var = jnp.mean(xc * xc, axis=-1, keepdims=True)
    return xc * lax.rsqrt(var + LN_EPS)


def _adaln_kernel(c_ref, w_ref, b_ref, o_ref):
    c = c_ref[...].astype(BF16)
    w = w_ref[0].astype(BF16)
    o_ref[0] = _dot(c, w) + b_ref[0]


def _adaln_all(c, ada_w, ada_b):
    depth, two, d, d3 = ada_w.shape
    n = depth * two
    b = c.shape[0]
    c_pad = jnp.zeros((8, d), F32).at[:b].set(c)
    w = ada_w.reshape(n, d, d3)
    bias = ada_b.reshape(n, 1, d3)
    tn = 768
    return pl.pallas_call(
        _adaln_kernel,
        grid=(n, d3 // tn),
        in_specs=[pl.BlockSpec((8, d), lambda i, j: (0, 0)),
                  pl.BlockSpec((1, d, tn), lambda i, j: (i, 0, j)),
                  pl.BlockSpec((1, 1, tn), lambda i, j: (i, 0, j))],
        out_specs=pl.BlockSpec((1, 8, tn), lambda i, j: (i, 0, j)),
        out_shape=jax.ShapeDtypeStruct((n, 8, d3), F32),
        compiler_params=_params("parallel", "parallel"),
        name="adaln",
    )(c_pad, w, bias)


def _modulate_kernel(x_ref, sh_ref, sc_ref, o_ref):
    y = _layer_norm(x_ref[...])
    o_ref[...] = (y * (1.0 + sc_ref[0]) + sh_ref[0]).astype(o_ref.dtype)


def _modulate(x, shift, scale, seq):
    t, d = x.shape
    tm = 512
    per_b = seq // tm
    return pl.pallas_call(
        _modulate_kernel,
        grid=(t // tm,),
        in_specs=[pl.BlockSpec((tm, d), lambda i: (i, 0)),
                  pl.BlockSpec((1, 1, d), lambda i: (i // per_b, 0, 0)),
                  pl.BlockSpec((1, 1, d), lambda i: (i // per_b, 0, 0))],
        out_specs=pl.BlockSpec((tm, d), lambda i: (i, 0)),
        out_shape=jax.ShapeDtypeStruct((t, d), BF16),
        compiler_params=_params("parallel"),
        name="modulate",
    )(x, shift, scale)


def _postnorm_kernel(x_ref, y_ref, gate_ref, g_ref, b_ref, o_ref, *, alpha):
    u = alpha * x_ref[...] + (1.0 + gate_ref[0]) * y_ref[...]
    o_ref[...] = _layer_norm(u) * g_ref[...] + b_ref[...]


def _postnorm(x, y, gate, g, b, seq, alpha):
    t, d = x.shape
    tm = 512
    per_b = seq // tm
    return pl.pallas_call(
        functools.partial(_postnorm_kernel, alpha=alpha),
        grid=(t // tm,),
        in_specs=[pl.BlockSpec((tm, d), lambda i: (i, 0)),
                  pl.BlockSpec((tm, d), lambda i: (i, 0)),
                  pl.BlockSpec((1, 1, d), lambda i: (i // per_b, 0, 0)),
                  pl.BlockSpec((1, d), lambda i: (0, 0)),
                  pl.BlockSpec((1, d), lambda i: (0, 0))],
        out_specs=pl.BlockSpec((tm, d), lambda i: (i, 0)),
        out_shape=jax.ShapeDtypeStruct((t, d), F32),
        compiler_params=_params("parallel"),
        name="postnorm",
    )(x, y, gate, g.reshape(1, d), b.reshape(1, d))


def _mm_kernel(a_ref, b_ref, o_ref):
    o_ref[...] = _dot(a_ref[...], b_ref[...]).astype(o_ref.dtype)


def _mm(a, w, out_dtype, tm=512, tn=1024):
    m, k = a.shape
    n = w.shape[1]
    tm = min(tm, m)
    tn = min(tn, n)
    return pl.pallas_call(
        _mm_kernel,
        grid=(n // tn, m // tm),
        in_specs=[pl.BlockSpec((tm, k), lambda j, i: (i, 0)),
                  pl.BlockSpec((k, tn), lambda j, i: (0, j))],
        out_specs=pl.BlockSpec((tm, tn), lambda j, i: (i, j)),
        out_shape=jax.ShapeDtypeStruct((m, n), out_dtype),
        compiler_params=_params("parallel", "parallel"),
        name="mm",
    )(a, w)


def _mm2_kernel(a1_ref, a2_ref, w1_ref, w2_ref, o_ref):
    acc = _dot(a1_ref[...].astype(BF16), w1_ref[...])
    acc = acc + _dot(a2_ref[...].astype(BF16), w2_ref[...])
    o_ref[...] = acc


def _mm2(a1, a2, w, tm=512, tn=1024):
    m, k = a1.shape
    n = w.shape[1]
    tm = min(tm, m)
    tn = min(tn, n)
    return pl.pallas_call(
        _mm2_kernel,
        grid=(n // tn, m // tm),
        in_specs=[pl.BlockSpec((tm, k), lambda j, i: (i, 0)),
                  pl.BlockSpec((tm, k), lambda j, i: (i, 0)),
                  pl.BlockSpec((k, tn), lambda j, i: (0, j)),
                  pl.BlockSpec((k, tn), lambda j, i: (1, j))],
        out_specs=pl.BlockSpec((tm, tn), lambda j, i: (i, j)),
        out_shape=jax.ShapeDtypeStruct((m, n), F32),
        compiler_params=_params("parallel", "parallel"),
        name="mm2",
    )(a1, a2, w, w)


def _fourier_proj_kernel(z_ref, w_ref, a_ref, b_ref):
    r = _dot(z_ref[...], w_ref[0])
    a_ref[...] = r[:, :GROUP_W].astype(a_ref.dtype)
    b_ref[...] = r[:, GROUP_W:].astype(b_ref.dtype)


def _fourier_proj(z, w_ab):
    t = z.shape[0]
    tm = 1024
    wid = N_FOURIER * GROUP_W
    return pl.pallas_call(
        _fourier_proj_kernel,
        grid=(N_FOURIER, t // tm),
        in_specs=[pl.BlockSpec((tm, GROUP_W), lambda g, i: (i, g)),
                  pl.BlockSpec((1, GROUP_W, 2 * GROUP_W), lambda g, i: (g, 0, 0))],
        out_specs=[pl.BlockSpec((tm, GROUP_W), lambda g, i: (i, g)),
                   pl.BlockSpec((tm, GROUP_W), lambda g, i: (i, g))],
        out_shape=[jax.ShapeDtypeStruct((t, wid), BF16), jax.ShapeDtypeStruct((t, wid), BF16)],
        compiler_params=_params("parallel", "parallel"),
        name="fourier_proj",
    )(z, w_ab)


def _fft1_kernel(a_ref, b_ref, c_ref, s_ref, twc_ref, tws_ref, tr_ref, ti_ref):
    a = a_ref[0]
    b = b_ref[0]
    c = c_ref[...]
    s = s_ref[...]
    tr = _dot(c, a) + _dot(s, b)
    ti = _dot(c, b) - _dot(s, a)
    reps = a.shape[1] // LANES
    wc = jnp.concatenate([twc_ref[0]] * reps, axis=1)
    ws = jnp.concatenate([tws_ref[0]] * reps, axis=1)
    tr_ref[0] = (tr * wc + ti * ws).astype(tr_ref.dtype)
    ti_ref[0] = (ti * wc - tr * ws).astype(ti_ref.dtype)


def _fft2_kernel(tr_ref, ti_ref, lr_ref, li_ref, o_ref):
    kb, n2, cols = tr_ref.shape[1:]
    tr = tr_ref[0].reshape(kb * n2, cols)
    ti = ti_ref[0].reshape(kb * n2, cols)
    res = _dot(lr_ref[...], tr) + _dot(li_ref[...], ti)
    o_ref[0] = res.reshape(n2, kb, cols)


def _seq_dft_real(a, b, batch, seq):
    wid = a.shape[1]
    n1 = FFT_N1
    n2 = seq // n1
    kb = FFT_K1_BLOCK
    k = np.arange(n1)
    ang1 = 2.0 * np.pi * np.outer(k, k) / n1
    c1 = jnp.asarray(np.cos(ang1), BF16)
    s1 = jnp.asarray(np.sin(ang1), BF16)
    angt = 2.0 * np.pi * np.outer(np.arange(n2), k) / seq
    twc = jnp.asarray(np.broadcast_to(np.cos(angt)[:, :, None], (n2, n1, LANES)), F32)
    tws = jnp.asarray(np.broadcast_to(np.sin(angt)[:, :, None], (n2, n1, LANES)), F32)
    a3 = a.reshape(batch, n1, n2 * wid)
    b3 = b.reshape(batch, n1, n2 * wid)
    tr, ti = pl.pallas_call(
        _fft1_kernel,
        grid=(batch, n2),
        in_specs=[pl.BlockSpec((1, n1, wid), lambda bb, j: (bb, 0, j)),
                  pl.BlockSpec((1, n1, wid), lambda bb, j: (bb, 0, j)),
                  pl.BlockSpec((n1, n1), lambda bb, j: (0, 0)),
                  pl.BlockSpec((n1, n1), lambda bb, j: (0, 0)),
                  pl.BlockSpec((1, n1, LANES), lambda bb, j: (j, 0, 0)),
                  pl.BlockSpec((1, n1, LANES), lambda bb, j: (j, 0, 0))],
        out_specs=[pl.BlockSpec((1, n1, wid), lambda bb, j: (bb, 0, j)),
                   pl.BlockSpec((1, n1, wid), lambda bb, j: (bb, 0, j))],
        out_shape=[jax.ShapeDtypeStruct((batch, n1, n2 * wid), BF16)] * 2,
        compiler_params=_params("parallel", "parallel"),
        name="fft_stage1",
    )(a3, b3, c1, s1, twc, tws)
    k2 = np.arange(n2)
    ang2 = 2.0 * np.pi * np.outer(k2, k2) / n2
    eye = np.eye(kb)
    lr = np.einsum('kn,jJ->kjJn', np.cos(ang2), eye).reshape(n2 * kb, kb * n2)
    li = np.einsum('kn,jJ->kjJn', np.sin(ang2), eye).reshape(n2 * kb, kb * n2)
    tr4 = tr.reshape(batch, n1, n2, wid)
    ti4 = ti.reshape(batch, n1, n2, wid)
    out = pl.pallas_call(
        _fft2_kernel,
        grid=(batch, n1 // kb),
        in_specs=[pl.BlockSpec((1, kb, n2, wid), lambda bb, j: (bb, j, 0, 0)),
                  pl.BlockSpec((1, kb, n2, wid), lambda bb, j: (bb, j, 0, 0)),
                  pl.BlockSpec((n2 * kb, kb * n2), lambda bb, j: (0, 0)),
                  pl.BlockSpec((n2 * kb, kb * n2), lambda bb, j: (0, 0))],
        out_specs=pl.BlockSpec((1, n2, kb, wid), lambda bb, j: (bb, 0, j, 0)),
        out_shape=jax.ShapeDtypeStruct((batch, n2, n1, wid), F32),
        compiler_params=_params("parallel", "parallel"),
        name="fft_stage2",
    )(tr4, ti4, jnp.asarray(lr, BF16), jnp.asarray(li, BF16))
    return out.reshape(batch * seq, wid)


def _fourier_weights(w_fourier, seq):
    g, cg, _ = w_fourier.shape
    k = np.arange(cg)
    ang = 2.0 * np.pi * np.outer(k, k) / cg
    norm = 1.0 / math.sqrt(seq * cg)
    cs = jnp.asarray(np.concatenate([np.cos(ang), -np.sin(ang)], axis=0) * norm, BF16)
    wf = jnp.transpose(w_fourier, (1, 0, 2)).reshape(cg, g * cg).astype(BF16)
    r = _mm(cs, wf, F32)
    r = r.reshape(2, cg, g, cg)
    return jnp.concatenate([r[0], r[1]], axis=-1).transpose(1, 0, 2).astype(BF16)


POOL_TM = 256
POOL_PAD = 128


def _pool_kernel(z_ref, band_ref, w_ref, ps_ref, o_ref, zp_ref, *, win, seq):
    cg = z_ref.shape[2]
    zp_ref[0:POOL_PAD, :] = jnp.zeros((POOL_PAD, cg), BF16)
    zp_ref[seq + POOL_PAD:seq + 2 * POOL_PAD, :] = jnp.zeros((POOL_PAD, cg), BF16)
    zp_ref[POOL_PAD:seq + POOL_PAD, :] = z_ref[0]
    band = band_ref[...]
    w = w_ref[0]
    ps = ps_ref[0]
    half = win // 2

    def body(i, carry):
        t0 = pl.multiple_of(i * POOL_TM, POOL_TM)
        slab = zp_ref[pl.ds(t0, POOL_TM + 2 * POOL_PAD), :]
        sums = _dot(band, slab)
        t = t0 + lax.broadcasted_iota(jnp.int32, (POOL_TM, cg), 0)
        cnt = (jnp.minimum(t + half, seq) - jnp.maximum(t - half, 0)).astype(F32)
        zc = zp_ref[pl.ds(t0 + POOL_PAD, POOL_TM), :].astype(F32)
        p = sums / cnt - zc
        y = _dot(p.astype(BF16), w) * ps
        o_ref[0, pl.ds(t0, POOL_TM), :] = y.astype(o_ref.dtype)
        return carry

    lax.fori_loop(0, seq // POOL_TM, body, 0)


def _pool_group(z3, w_pool, pool_scale, g, batch, seq):
    win = POOL_WINDOWS[g]
    half = win // 2
    tau = np.arange(POOL_TM)[:, None]
    kap = np.arange(POOL_TM + 2 * POOL_PAD)[None, :]
    off = kap - POOL_PAD - tau
    band = jnp.asarray(((off >= -half) & (off < half)).astype(np.float32), BF16)
    col = N_FOURIER + g
    return pl.pallas_call(
        functools.partial(_pool_kernel, win=win, seq=seq),
        grid=(batch,),
        in_specs=[pl.BlockSpec((1, seq, GROUP_W), lambda bb: (bb, 0, col)),
                  pl.BlockSpec((POOL_TM, POOL_TM + 2 * POOL_PAD), lambda bb: (0, 0)),
                  pl.BlockSpec((1, GROUP_W, GROUP_W), lambda bb: (g, 0, 0)),
                  pl.BlockSpec((1, 1, GROUP_W), lambda bb: (g, 0, 0))],
        out_specs=pl.BlockSpec((1, seq, GROUP_W), lambda bb: (bb, 0, 0)),
        out_shape=jax.ShapeDtypeStruct((batch, seq, GROUP_W), BF16),
        scratch_shapes=[pltpu.VMEM((seq + 2 * POOL_PAD, GROUP_W), BF16)],
        compiler_params=_params("parallel"),
        name=f"pool_w{win}",
    )(z3, band, w_pool, pool_scale)


def _fourier_pool_mixer(h, w_in, w_fourier, w_pool, pool_scale, w_out, batch, seq):
    t = h.shape[0]
    z = _mm(h, w_in.astype(BF16), BF16)
    a, b = _fourier_proj(z, _fourier_weights(w_fourier, seq))
    yf = _seq_dft_real(a, b, batch, seq)
    z3 = z.reshape(batch, seq, z.shape[1])
    wp = w_pool.astype(BF16)
    ps = pool_scale.reshape(N_POOL, 1, GROUP_W)
    yp = jnp.concatenate([_pool_group(z3, wp, ps, g, batch, seq) for g in range(N_POOL)], axis=-1)
    return _mm2(yf, yp.reshape(t, N_POOL * GROUP_W), w_out.astype(BF16))


def _qk_prep_kernel(x_ref, g_ref, cos_ref, sin_ref, o_ref):
    x = x_ref[...].astype(F32)
    y = x * lax.rsqrt(jnp.mean(x * x, axis=-1, keepdims=True) + LN_EPS) * g_ref[0]
    lane = lax.broadcasted_iota(jnp.int32, y.shape, 1)
    first = (lane % (HEAD // 2)) < (HEAD // 4)
    partner = jnp.where(first, pltpu.roll(y, HEAD - HEAD // 4, 1), pltpu.roll(y, HEAD // 4, 1))
    o_ref[...] = (y * cos_ref[...] + partner * sin_ref[...]).astype(o_ref.dtype)


def _rope_tables(seq):
    quarter = HEAD // 4
    inv = ROPE_THETA ** (-np.arange(quarter, dtype=np.float64) / quarter)
    t = np.arange(seq)
    ang_r = (t // GRID_COLS)[:, None] * inv[None, :]
    ang_c = (t % GRID_COLS)[:, None] * inv[None, :]
    cos = np.concatenate([np.cos(ang_r), np.cos(ang_r), np.cos(ang_c), np.cos(ang_c)], axis=1)
    sin = np.concatenate([-np.sin(ang_r), np.sin(ang_r), -np.sin(ang_c), np.sin(ang_c)], axis=1)
    return jnp.asarray(cos, F32), jnp.asarray(sin, F32)


def _qk_prep(proj, gains, seq):
    t = proj.shape[0]
    nh = gains.shape[0]
    tm = 1024
    per_b = seq // tm
    cos, sin = _rope_tables(seq)
    return pl.pallas_call(
        _qk_prep_kernel,
        grid=(t // tm, nh),
        in_specs=[pl.BlockSpec((tm, HEAD), lambda i, hh: (i, hh)),
                  pl.BlockSpec((1, 1, HEAD), lambda i, hh: (hh, 0, 0)),
                  pl.BlockSpec((tm, HEAD), lambda i, hh: (i % per_b, 0)),
                  pl.BlockSpec((tm, HEAD), lambda i, hh: (i % per_b, 0))],
        out_specs=pl.BlockSpec((tm, HEAD), lambda i, hh: (i, hh)),
        out_shape=jax.ShapeDtypeStruct((t, nh * HEAD), BF16),
        compiler_params=_params("parallel", "parallel"),
        name="qk_prep",
    )(proj, gains, cos, sin)


FLASH_TQ = 512
FLASH_TK = 512
FLASH_UNROLL = 8


def _flash_kernel(q_ref, k_ref, v_ref, o_ref):
    seq = k_ref.shape[0]
    tq = q_ref.shape[0]
    for hh in range(N_Q_HEADS // N_KV_HEADS):
        q = q_ref[:, hh * HEAD:(hh + 1) * HEAD]

        def body(c, carry):
            m, l, acc = carry
            c0 = pl.multiple_of(c * FLASH_TK, FLASH_TK)
            k = k_ref[pl.ds(c0, FLASH_TK), :]
            v = v_ref[pl.ds(c0, FLASH_TK), :]
            s = _dot_nt(q, k)
            m_new = jnp.maximum(m, jnp.max(s, axis=-1, keepdims=True))
            p = jnp.exp2(s - m_new)
            corr = jnp.exp2(m - m_new)
            l = l * corr + jnp.sum(p, axis=-1, keepdims=True)
            acc = acc * corr + _dot(p.astype(BF16), v)
            return m_new, l, acc

        init = (jnp.full((tq, 1), NEG_BIG, F32), jnp.zeros((tq, 1), F32), jnp.zeros((tq, HEAD), F32))
        m, l, acc = lax.fori_loop(0, seq // FLASH_TK, body, init, unroll=FLASH_UNROLL)
        o_ref[:, hh * HEAD:(hh + 1) * HEAD] = (acc / l).astype(o_ref.dtype)


def _gqa_attention(qk, proj, batch, seq):
    t = qk.shape[0]
    per_b = seq // FLASH_TQ
    gw = (N_Q_HEADS // N_KV_HEADS) * HEAD
    k_col = N_Q_HEADS
    v_col = N_Q_HEADS + N_KV_HEADS
    return pl.pallas_call(
        _flash_kernel,
        grid=(batch, N_KV_HEADS, per_b),
        in_specs=[pl.BlockSpec((FLASH_TQ, gw), lambda bb, kv, i: (bb * per_b + i, kv)),
                  pl.BlockSpec((seq, HEAD), lambda bb, kv, i: (bb, k_col + kv)),
                  pl.BlockSpec((seq, HEAD), lambda bb, kv, i: (bb, v_col + kv))],
        out_specs=pl.BlockSpec((FLASH_TQ, gw), lambda bb, kv, i: (bb * per_b + i, kv)),
        out_shape=jax.ShapeDtypeStruct((t, N_Q_HEADS * HEAD), BF16),
        compiler_params=_params("parallel", "parallel", "parallel"),
        name="gqa_flash",
    )(qk, qk, proj)


NA_ROW_BLOCK = 8


def _na_kernel(q_ref, k_ref, v_ref, bias_ref, o_ref, *, rows, scale):
    i = pl.program_id(2)
    span = NA_ROWS * GRID_COLS
    for rr in range(NA_ROW_BLOCK):
        r = i * NA_ROW_BLOCK + rr
        rs = jnp.clip(r - NA_ROWS // 2, 0, rows - NA_ROWS)
        dr0 = rs - r + (NA_ROWS - 1)
        k0 = pl.multiple_of(rs * GRID_COLS, GRID_COLS)
        q = q_ref[rr * GRID_COLS:(rr + 1) * GRID_COLS, :]
        k = k_ref[pl.ds(k0, span), :]
        v = v_ref[pl.ds(k0, span), :]
        s = _dot_nt(q, k) * scale + bias_ref[0, dr0]
        m = jnp.max(s, axis=-1, keepdims=True)
        p = jnp.exp(s - m)
        p = p / jnp.sum(p, axis=-1, keepdims=True)
        o_ref[rr * GRID_COLS:(rr + 1) * GRID_COLS, :] = _dot(p.astype(BF16), v).astype(o_ref.dtype)


def _na_bias_table(rpb):
    cols = np.arange(GRID_COLS)
    start = np.clip(cols - NA_COLS // 2, 0, GRID_COLS - NA_COLS)
    kc = np.arange(GRID_COLS)
    valid = (kc[None, :] >= start[:, None]) & (kc[None, :] < start[:, None] + NA_COLS)
    off = np.clip(kc[None, :] - cols[:, None] + (NA_COLS - 1), 0, 2 * NA_COLS - 2)
    tab = rpb[:, :, off]
    tab = jnp.where(jnp.asarray(valid)[None, None], tab, NEG_BIG)
    per_dr0 = [jnp.concatenate([tab[:, d + j] for j in range(NA_ROWS)], axis=-1) for d in range(NA_ROWS)]
    return jnp.stack(per_dr0, axis=1).astype(F32)


def _neighbourhood_attention(proj, rpb, batch, seq):
    t = proj.shape[0]
    rows = seq // GRID_COLS
    qb = NA_ROW_BLOCK * GRID_COLS
    per_b = seq // qb
    q_col = (N_Q_HEADS + 2 * N_KV_HEADS)
    k_col = q_col + N_NA_HEADS
    v_col = k_col + N_NA_HEADS
    bias = _na_bias_table(rpb)
    return pl.pallas_call(
        functools.partial(_na_kernel, rows=rows, scale=HEAD ** -0.5),
        grid=(batch, N_NA_HEADS, per_b),
        in_specs=[pl.BlockSpec((qb, HEAD), lambda bb, hh, i: (bb * per_b + i, q_col + hh)),
                  pl.BlockSpec((seq, HEAD), lambda bb, hh, i: (bb, k_col + hh)),
                  pl.BlockSpec((seq, HEAD), lambda bb, hh, i: (bb, v_col + hh)),
                  pl.BlockSpec((1, NA_ROWS, GRID_COLS, NA_ROWS * GRID_COLS), lambda bb, hh, i: (hh, 0, 0, 0))],
        out_specs=pl.BlockSpec((qb, HEAD), lambda bb, hh, i: (bb * per_b + i, hh)),
        out_shape=jax.ShapeDtypeStruct((t, N_NA_HEADS * HEAD), BF16),
        compiler_params=_params("parallel", "parallel", "parallel"),
        name="natten",
    )(proj, proj, proj, bias)


def _attention_mixer(h, w_in, q_norm, k_norm, rpb, w_out, batch, seq):
    proj = _mm(h, w_in.astype(BF16), BF16, tn=1152)
    q_gain = HEAD ** -0.5 * math.log2(math.e)
    gains = jnp.concatenate([jnp.tile(q_norm[None] * q_gain, (N_Q_HEADS, 1)),
                             jnp.tile(k_norm[None], (N_KV_HEADS, 1))], axis=0)
    qk = _qk_prep(proj, gains.reshape(N_Q_HEADS + N_KV_HEADS, 1, HEAD).astype(F32), seq)
    yc = _gqa_attention(qk, proj, batch, seq)
    yd = _neighbourhood_attention(proj, rpb, batch, seq)
    return _mm2(yc, yd, w_out.astype(BF16))


def _row(v, i, fill):
    rows = lax.broadcasted_iota(jnp.int32, v.shape, 0)
    return jnp.max(jnp.where(rows == i, v, fill), axis=0, keepdims=True)


def _top16(s, ids, big):
    n = s.shape[1]
    slot = lax.broadcasted_iota(jnp.int32, (PEER_TOPK, n), 0)
    vals = jnp.zeros((PEER_TOPK, n), F32)
    idxs = jnp.zeros((PEER_TOPK, n), jnp.int32)
    for it in range(PEER_TOPK):
        m = jnp.max(s, axis=0, keepdims=True)
        ix = jnp.min(jnp.where(s == m, ids, big), axis=0, keepdims=True)
        s = jnp.where(ids == ix, -jnp.inf, s)
        vals = jnp.where(slot == it, m, vals)
        idxs = jnp.where(slot == it, ix, idxs)
    return vals, idxs


def _pick(table, sel):
    out = jnp.zeros_like(table)
    for r in range(PEER_TOPK):
        out = jnp.where(sel == r, _row(table, r, -1), out)
    return out


def _peer_topk_kernel(q_ref, keys_ref, e1_ref, e2_ref, g_ref):
    tm = q_ref.shape[0]
    half = q_ref.shape[1] // 2
    k = PEER_TOPK
    s1 = _dot_nt(keys_ref[0, 0], q_ref[:, :half])
    s2 = _dot_nt(keys_ref[0, 1], q_ref[:, half:])
    key_ids = lax.broadcasted_iota(jnp.int32, (PEER_KEYS, tm), 0)
    v1, i1 = _top16(s1, key_ids, PEER_KEYS)
    v2, i2 = _top16(s2, key_ids, PEER_KEYS)
    i8 = lax.broadcasted_iota(jnp.int32, (8, tm), 0)
    i16 = lax.broadcasted_iota(jnp.int32, (k, tm), 0)
    ninf = -jnp.inf
    cand = [_row(v1, 0, ninf) + v2]
    ids = [i16]
    for i in (1, 2, 3):
        cand.append(_row(v1, i, ninf) + v2[0:8])
        ids.append(i * k + i8)
    cand.append(v1[8:16] + _row(v2, 0, ninf))
    ids.append((i8 + 8) * k)
    for j in (0, 1):
        cand.append(jnp.where(i8 < 4, ninf, v1[0:8] + _row(v2, j, ninf)))
        ids.append(i8 * k + j)
    cand = jnp.concatenate(cand, axis=0)
    ids = jnp.concatenate(ids, axis=0)
    sc, ci = _top16(cand, ids, k * k)
    e1 = _pick(i1, lax.shift_right_logical(ci, 4))
    e2 = _pick(i2, lax.bitwise_and(ci, k - 1))
    ex = jnp.exp(sc - _row(sc, 0, ninf))
    gates = ex / jnp.sum(ex, axis=0, keepdims=True)
    e1_ref[0] = e1.astype(F32)
    e2_ref[0] = e2.astype(F32)
    g_ref[0] = gates


def _peer_topk(q, sub_keys):
    t = q.shape[0]
    tm = 256
    qd = q.shape[1] // PEER_HEADS
    out = jax.ShapeDtypeStruct((PEER_HEADS, PEER_TOPK, t), F32)
    spec = pl.BlockSpec((1, PEER_TOPK, tm), lambda i, hh: (hh, 0, i))
    return pl.pallas_call(
        _peer_topk_kernel,
        grid=(t // tm, PEER_HEADS),
        in_specs=[pl.BlockSpec((tm, qd), lambda i, hh: (i, hh)),
                  pl.BlockSpec((1, 2, PEER_KEYS, qd // 2), lambda i, hh: (hh, 0, 0, 0))],
        out_specs=[spec, spec, spec],
        out_shape=[out, out, out],
        compiler_params=_params("parallel", "parallel"),
        name="peer_topk",
    )(q, sub_keys)


PEER_TM = 512
PEER_TE = 1024
PEER_SUB = 512
PEER_PASSES = 2
PEER_BUILD_UNROLL = 32
PEER_ROW_PAD = 8


def _peer_experts_kernel(h_ref, e1_ref, e2_ref, gt_ref, u_ref, v_ref, o_ref, gmat, e1s, e2s, gs):
    p = pl.program_id(1)
    c = pl.program_id(2)
    tm = h_ref.shape[0]
    te = u_ref.shape[0]
    slots = PEER_HEADS * PEER_TOPK
    pitch = gmat.shape[0] // tm
    kpp = pitch - PEER_ROW_PAD

    @pl.when((p == 0) & (c == 0))
    def _():
        o_ref[...] = jnp.zeros_like(o_ref)
        e1s[...] = e1_ref[...].reshape(slots, tm).T
        e2s[...] = e2_ref[...].reshape(slots, tm).T
        gs[...] = gt_ref[...].reshape(slots, tm).T

    @pl.when(c == 0)
    def _():
        key1 = (lax.broadcasted_iota(jnp.int32, (kpp, slots), 0) + p * kpp).astype(F32)
        key2 = lax.broadcasted_iota(jnp.int32, (PEER_KEYS, slots), 0).astype(F32)

        def body(t, carry):
            r1 = e1s[pl.ds(t, 1), :]
            r2 = e2s[pl.ds(t, 1), :]
            rg = gs[pl.ds(t, 1), :]
            a = jnp.where(key1 == r1, 1.0, 0.0).astype(BF16)
            b = jnp.where(key2 == r2, rg, 0.0).astype(BF16)
            gmat[pl.ds(pl.multiple_of(t * pitch, PEER_ROW_PAD), kpp), :] = _dot_nt(a, b)
            return carry

        lax.fori_loop(0, tm, body, 0, unroll=PEER_BUILD_UNROLL)

    per_step = te // PEER_KEYS
    per_sub = PEER_SUB // PEER_KEYS
    for sb in range(te // PEER_SUB):
        rows = slice(sb * PEER_SUB, (sb + 1) * PEER_SUB)
        a = _dot_nt(h_ref[...], u_ref[rows, :])
        act = 0.5 * a * (1.0 + lax.erf(a * (1.0 / math.sqrt(2.0))))
        g = jnp.concatenate([gmat[pl.ds(c * per_step + sb * per_sub + j, tm, stride=pitch), :]
                             for j in range(per_sub)], axis=1)
        o_ref[...] += _dot((act * g).astype(BF16), v_ref[rows, :])


def _peer_experts(h, e1, e2, gates, u, v):
    t, d = h.shape
    n_exp = u.shape[0]
    tm = min(PEER_TM, t)
    te = PEER_TE
    kpp = PEER_KEYS // PEER_PASSES
    cpp = n_exp // PEER_PASSES // te
    slots = PEER_HEADS * PEER_TOPK
    sel = pl.BlockSpec((PEER_HEADS, PEER_TOPK, tm), lambda i, p, c: (0, 0, i))
    return pl.pallas_call(
        _peer_experts_kernel,
        grid=(t // tm, PEER_PASSES, cpp),
        in_specs=[pl.BlockSpec((tm, d), lambda i, p, c: (i, 0)),
                  sel, sel, sel,
                  pl.BlockSpec((te, d), lambda i, p, c: (p * cpp + c, 0)),
                  pl.BlockSpec((te, d), lambda i, p, c: (p * cpp + c, 0))],
        out_specs=pl.BlockSpec((tm, d), lambda i, p, c: (i, 0)),
        out_shape=jax.ShapeDtypeStruct((t, d), F32),
        scratch_shapes=[pltpu.VMEM((tm * (kpp + PEER_ROW_PAD), PEER_KEYS), F32)]
        + [pltpu.VMEM((tm, slots), F32)] * 3,
        compiler_params=_params("parallel", "arbitrary", "arbitrary"),
        name="peer_experts",
    )(h, e1, e2, gates, u, v)


def _peer(h, w_q, sub_keys, u, v):
    q = _mm(h, w_q.astype(BF16), BF16)
    e1, e2, gates = _peer_topk(q, sub_keys.astype(BF16))
    return _peer_experts(h, e1, e2, gates, u.astype(BF16), v.astype(BF16))


def kernel(x, c, ada_w, ada_b, ln_g, ln_b, fp_w_in, fp_w_fourier, fp_w_pool, fp_pool_scale, fp_w_out,
           at_w_in, at_q_norm, at_k_norm, at_rpb, at_w_out, peer_w_q, peer_sub_keys, peer_u, peer_v):
    batch, seq, d = x.shape
    depth = ada_w.shape[0]
    alpha = (2.0 * depth) ** 0.25
    xt = x.reshape(batch * seq, d)
    ada = _adaln_all(c, ada_w, ada_b)[:, :batch]
    shift = ada[:, :, None, :d]
    scale = ada[:, :, None, d:2 * d]
    gate = ada[:, :, None, 2 * d:]
    for l in range(depth):
        i = l // 2
        n = 2 * l
        h = _modulate(xt, shift[n], scale[n], seq)
        if l % 2 == 0:
            y = _fourier_pool_mixer(h, fp_w_in[i], fp_w_fourier[i], fp_w_pool[i], fp_pool_scale[i],
                                    fp_w_out[i], batch, seq)
        else:
            y = _attention_mixer(h, at_w_in[i], at_q_norm[i], at_k_norm[i], at_rpb[i], at_w_out[i],
                                 batch, seq)
        xt = _postnorm(xt, y, gate[n], ln_g[l, 0], ln_b[l, 0], seq, alpha)
        n = 2 * l + 1
        h = _modulate(xt, shift[n], scale[n], seq)
        y = _peer(h, peer_w_q[l], peer_sub_keys[l], peer_u[l], peer_v[l])
        xt = _postnorm(xt, y, gate[n], ln_g[l, 1], ln_b[l, 1], seq, alpha)
    return xt.reshape(batch, seq, d)
```

```python
import functools
import math

import numpy as np
import jax
import jax.numpy as jnp
from jax import lax
from jax.experimental import pallas as pl
from jax.experimental.pallas import tpu as pltpu

F32 = jnp.float32
BF16 = jnp.bfloat16

GRID_COLS = 64
HEAD = 128
GROUP_W = 256
N_FOURIER = 4
N_POOL = 4
POOL_WINDOWS = (2, 4, 8, 16)
N_Q_HEADS = 8
N_KV_HEADS = 2
N_NA_HEADS = 8
NA_ROWS = 8
NA_COLS = 16
ROPE_THETA = 10000.0
PEER_HEADS = 8
PEER_KEYS = 128
PEER_TOPK = 16
LN_EPS = 1e-6
FFT_N1 = 128
FFT_K1_BLOCK = 8

VMEM_LIMIT_BYTES = 56 * 1024 * 1024
LANES = 128
NEG_BIG = -1e30


def _params(*sem):
    return pltpu.CompilerParams(dimension_semantics=sem, vmem_limit_bytes=VMEM_LIMIT_BYTES)


def _dot(a, b):
    return jnp.dot(a, b, preferred_element_type=F32)


def _dot_nt(a, b):
    return lax.dot_general(a, b, (((1,), (1,)), ((), ())), preferred_element_type=F32)


def _layer_norm(x):
    mu = jnp.mean(x, axis=-1, keepdims=True)
    xc = x - mu
    var = jnp.mean(xc * xc, axis=-1, keepdims=True)
    return xc * lax.rsqrt(var + LN_EPS)


def _adaln_kernel(c_ref, w_ref, b_ref, o_ref):
    c = c_ref[...].astype(BF16)
    w = w_ref[0].astype(BF16)
    o_ref[0] = _dot(c, w) + b_ref[0]


def _adaln_all(c, ada_w, ada_b):
    depth, two, d, d3 = ada_w.shape
    n = depth * two
    b = c.shape[0]
    c_pad = jnp.zeros((8, d), F32).at[:b].set(c)
    w = ada_w.reshape(n, d, d3)
    bias = ada_b.reshape(n, 1, d3)
    tn = 768
    return pl.pallas_call(
        _adaln_kernel,
        grid=(n, d3 // tn),
        in_specs=[pl.BlockSpec((8, d), lambda i, j: (0, 0)),
                  pl.BlockSpec((1, d, tn), lambda i, j: (i, 0, j)),
                  pl.BlockSpec((1, 1, tn), lambda i, j: (i, 0, j))],
        out_specs=pl.BlockSpec((1, 8, tn), lambda i, j: (i, 0, j)),
        out_shape=jax.ShapeDtypeStruct((n, 8, d3), F32),
        compiler_params=_params("parallel", "parallel"),
        name="adaln",
    )(c_pad, w, bias)


def _modulate_kernel(x_ref, sh_ref, sc_ref, o_ref):
    y = _layer_norm(x_ref[...])
    o_ref[...] = (y * (1.0 + sc_ref[0]) + sh_ref[0]).astype(o_ref.dtype)


def _modulate(x, shift, scale, seq):
    t, d = x.shape
    tm = 512
    per_b = seq // tm
    return pl.pallas_call(
        _modulate_kernel,
        grid=(t // tm,),
        in_specs=[pl.BlockSpec((tm, d), lambda i: (i, 0)),
                  pl.BlockSpec((1, 1, d), lambda i: (i // per_b, 0, 0)),
                  pl.BlockSpec((1, 1, d), lambda i: (i // per_b, 0, 0))],
        out_specs=pl.BlockSpec((tm, d), lambda i: (i, 0)),
        out_shape=jax.ShapeDtypeStruct((t, d), BF16),
        compiler_params=_params("parallel"),
        name="modulate",
    )(x, shift, scale)


def _postnorm_kernel(x_ref, y_ref, gate_ref, g_ref, b_ref, o_ref, *, alpha):
    u = alpha * x_ref[...] + (1.0 + gate_ref[0]) * y_ref[...]
    o_ref[...] = _layer_norm(u) * g_ref[...] + b_ref[...]


def _postnorm(x, y, gate, g, b, seq, alpha):
    t, d = x.shape
    tm = 512
    per_b = seq // tm
    return pl.pallas_call(
        functools.partial(_postnorm_kernel, alpha=alpha),
        grid=(t // tm,),
        in_specs=[pl.BlockSpec((tm, d), lambda i: (i, 0)),
                  pl.BlockSpec((tm, d), lambda i: (i, 0)),
                  pl.BlockSpec((1, 1, d), lambda i: (i // per_b, 0, 0)),
                  pl.BlockSpec((1, d), lambda i: (0, 0)),
                  pl.BlockSpec((1, d), lambda i: (0, 0))],
        out_specs=pl.BlockSpec((tm, d), lambda i: (i, 0)),
        out_shape=jax.ShapeDtypeStruct((t, d), F32),
        compiler_params=_params("parallel"),
        name="postnorm",
    )(x, y, gate, g.reshape(1, d), b.reshape(1, d))


def _postnorm_modulate_kernel(x_ref, y_ref, gate_ref, g_ref, b_ref, sh_ref, sc_ref, o_ref, h_ref, *, alpha):
    u = alpha * x_ref[...] + (1.0 + gate_ref[0]) * y_ref[...]
    xn = _layer_norm(u) * g_ref[...] + b_ref[...]
    o_ref[...] = xn
    h_ref[...] = (_layer_norm(xn) * (1.0 + sc_ref[0]) + sh_ref[0]).astype(h_ref.dtype)


def _postnorm_modulate(x, y, gate, g, b, shift, scale, seq, alpha):
    t, d = x.shape
    tm = 512
    per_b = seq // tm
    row = pl.BlockSpec((tm, d), lambda i: (i, 0))
    per_batch = pl.BlockSpec((1, 1, d), lambda i: (i // per_b, 0, 0))
    vec = pl.BlockSpec((1, d), lambda i: (0, 0))
    return pl.pallas_call(
        functools.partial(_postnorm_modulate_kernel, alpha=alpha),
        grid=(t // tm,),
        in_specs=[row, row, per_batch, vec, vec, per_batch, per_batch],
        out_specs=[row, row],
        out_shape=[jax.ShapeDtypeStruct((t, d), F32), jax.ShapeDtypeStruct((t, d), BF16)],
        compiler_params=_params("parallel"),
        name="postnorm_modulate",
    )(x, y, gate, g.reshape(1, d), b.reshape(1, d), shift, scale)


def _mm_kernel(a_ref, b_ref, o_ref):
    o_ref[...] = _dot(a_ref[...], b_ref[...]).astype(o_ref.dtype)


def _mm(a, w, out_dtype, tm=512, tn=1024):
    m, k = a.shape
    n = w.shape[1]
    tm = min(tm, m)
    tn = min(tn, n)
    return pl.pallas_call(
        _mm_kernel,
        grid=(n // tn, m // tm),
        in_specs=[pl.BlockSpec((tm, k), lambda j, i: (i, 0)),
                  pl.BlockSpec((k, tn), lambda j, i: (0, j))],
        out_specs=pl.BlockSpec((tm, tn), lambda j, i: (i, j)),
        out_shape=jax.ShapeDtypeStruct((m, n), out_dtype),
        compiler_params=_params("parallel", "parallel"),
        name="mm",
    )(a, w)


def _mm2_kernel(a1_ref, a2_ref, w1_ref, w2_ref, o_ref):
    acc = _dot(a1_ref[...].astype(BF16), w1_ref[...])
    acc = acc + _dot(a2_ref[...].astype(BF16), w2_ref[...])
    o_ref[...] = acc


def _mm2(a1, a2, w, tm=512, tn=1024):
    m, k = a1.shape
    n = w.shape[1]
    tm = min(tm, m)
    tn = min(tn, n)
    return pl.pallas_call(
        _mm2_kernel,
        grid=(n // tn, m // tm),
        in_specs=[pl.BlockSpec((tm, k), lambda j, i: (i, 0)),
                  pl.BlockSpec((tm, k), lambda j, i: (i, 0)),
                  pl.BlockSpec((k, tn), lambda j, i: (0, j)),
                  pl.BlockSpec((k, tn), lambda j, i: (1, j))],
        out_specs=pl.BlockSpec((tm, tn), lambda j, i: (i, j)),
        out_shape=jax.ShapeDtypeStruct((m, n), F32),
        compiler_params=_params("parallel", "parallel"),
        name="mm2",
    )(a1, a2, w, w)


def _fourier_proj_kernel(z_ref, w_ref, a_ref, b_ref):
    r = _dot(z_ref[...], w_ref[0])
    a_ref[...] = r[:, :GROUP_W].astype(a_ref.dtype)
    b_ref[...] = r[:, GROUP_W:].astype(b_ref.dtype)


def _fourier_proj(z, w_ab):
    t = z.shape[0]
    tm = 1024
    wid = N_FOURIER * GROUP_W
    return pl.pallas_call(
        _fourier_proj_kernel,
        grid=(N_FOURIER, t // tm),
        in_specs=[pl.BlockSpec((tm, GROUP_W), lambda g, i: (i, g)),
                  pl.BlockSpec((1, GROUP_W, 2 * GROUP_W), lambda g, i: (g, 0, 0))],
        out_specs=[pl.BlockSpec((tm, GROUP_W), lambda g, i: (i, g)),
                   pl.BlockSpec((tm, GROUP_W), lambda g, i: (i, g))],
        out_shape=[jax.ShapeDtypeStruct((t, wid), BF16), jax.ShapeDtypeStruct((t, wid), BF16)],
        compiler_params=_params("parallel", "parallel"),
        name="fourier_proj",
    )(z, w_ab)


def _fft1_kernel(a_ref, b_ref, c_ref, s_ref, twc_ref, tws_ref, tr_ref, ti_ref):
    a = a_ref[0]
    b = b_ref[0]
    c = c_ref[...]
    s = s_ref[...]
    tr = _dot(c, a) + _dot(s, b)
    ti = _dot(c, b) - _dot(s, a)
    reps = a.shape[1] // LANES
    wc = jnp.concatenate([twc_ref[0]] * reps, axis=1)
    ws = jnp.concatenate([tws_ref[0]] * reps, axis=1)
    tr_ref[0] = (tr * wc + ti * ws).astype(tr_ref.dtype)
    ti_ref[0] = (ti * wc - tr * ws).astype(ti_ref.dtype)


def _fft2_kernel(tr_ref, ti_ref, lr_ref, li_ref, o_ref):
    kb, n2, cols = tr_ref.shape[1:]
    tr = tr_ref[0].reshape(kb * n2, cols)
    ti = ti_ref[0].reshape(kb * n2, cols)
    res = _dot(lr_ref[...], tr) + _dot(li_ref[...], ti)
    o_ref[0] = res.reshape(n2, kb, cols)


def _seq_dft_real(a, b, batch, seq):
    wid = a.shape[1]
    n1 = FFT_N1
    n2 = seq // n1
    kb = FFT_K1_BLOCK
    k = np.arange(n1)
    ang1 = 2.0 * np.pi * np.outer(k, k) / n1
    c1 = jnp.asarray(np.cos(ang1), BF16)
    s1 = jnp.asarray(np.sin(ang1), BF16)
    angt = 2.0 * np.pi * np.outer(np.arange(n2), k) / seq
    twc = jnp.asarray(np.broadcast_to(np.cos(angt)[:, :, None], (n2, n1, LANES)), F32)
    tws = jnp.asarray(np.broadcast_to(np.sin(angt)[:, :, None], (n2, n1, LANES)), F32)
    a3 = a.reshape(batch, n1, n2 * wid)
    b3 = b.reshape(batch, n1, n2 * wid)
    tr, ti = pl.pallas_call(
        _fft1_kernel,
        grid=(batch, n2),
        in_specs=[pl.BlockSpec((1, n1, wid), lambda bb, j: (bb, 0, j)),
                  pl.BlockSpec((1, n1, wid), lambda bb, j: (bb, 0, j)),
                  pl.BlockSpec((n1, n1), lambda bb, j: (0, 0)),
                  pl.BlockSpec((n1, n1), lambda bb, j: (0, 0)),
                  pl.BlockSpec((1, n1, LANES), lambda bb, j: (j, 0, 0)),
                  pl.BlockSpec((1, n1, LANES), lambda bb, j: (j, 0, 0))],
        out_specs=[pl.BlockSpec((1, n1, wid), lambda bb, j: (bb, 0, j)),
                   pl.BlockSpec((1, n1, wid), lambda bb, j: (bb, 0, j))],
        out_shape=[jax.ShapeDtypeStruct((batch, n1, n2 * wid), BF16)] * 2,
        compiler_params=_params("parallel", "parallel"),
        name="fft_stage1",
    )(a3, b3, c1, s1, twc, tws)
    k2 = np.arange(n2)
    ang2 = 2.0 * np.pi * np.outer(k2, k2) / n2
    eye = np.eye(kb)
    lr = np.einsum('kn,jJ->kjJn', np.cos(ang2), eye).reshape(n2 * kb, kb * n2)
    li = np.einsum('kn,jJ->kjJn', np.sin(ang2), eye).reshape(n2 * kb, kb * n2)
    tr4 = tr.reshape(batch, n1, n2, wid)
    ti4 = ti.reshape(batch, n1, n2, wid)
    out = pl.pallas_call(
        _fft2_kernel,
        grid=(batch, n1 // kb),
        in_specs=[pl.BlockSpec((1, kb, n2, wid), lambda bb, j: (bb, j, 0, 0)),
                  pl.BlockSpec((1, kb, n2, wid), lambda bb, j: (bb, j, 0, 0)),
                  pl.BlockSpec((n2 * kb, kb * n2), lambda bb, j: (0, 0)),
                  pl.BlockSpec((n2 * kb, kb * n2), lambda bb, j: (0, 0))],
        out_specs=pl.BlockSpec((1, n2, kb, wid), lambda bb, j: (bb, 0, j, 0)),
        out_shape=jax.ShapeDtypeStruct((batch, n2, n1, wid), F32),
        compiler_params=_params("parallel", "parallel"),
        name="fft_stage2",
    )(tr4, ti4, jnp.asarray(lr, BF16), jnp.asarray(li, BF16))
    return out.reshape(batch * seq, wid)


def _fourier_weights(w_fourier, seq):
    g, cg, _ = w_fourier.shape
    k = np.arange(cg)
    ang = 2.0 * np.pi * np.outer(k, k) / cg
    norm = 1.0 / math.sqrt(seq * cg)
    cs = jnp.asarray(np.concatenate([np.cos(ang), -np.sin(ang)], axis=0) * norm, BF16)
    wf = jnp.transpose(w_fourier, (1, 0, 2)).reshape(cg, g * cg).astype(BF16)
    r = _mm(cs, wf, F32)
    r = r.reshape(2, cg, g, cg)
    return jnp.concatenate([r[0], r[1]], axis=-1).transpose(1, 0, 2).astype(BF16)


POOL_TM = 256
POOL_PAD = 128


def _pool_kernel(z_ref, band_ref, w_ref, ps_ref, o_ref, zp_ref, *, win, seq):
    cg = z_ref.shape[2]
    zp_ref[0:POOL_PAD, :] = jnp.zeros((POOL_PAD, cg), BF16)
    zp_ref[seq + POOL_PAD:seq + 2 * POOL_PAD, :] = jnp.zeros((POOL_PAD, cg), BF16)
    zp_ref[POOL_PAD:seq + POOL_PAD, :] = z_ref[0]
    band = band_ref[...]
    w = w_ref[0]
    ps = ps_ref[0]
    half = win // 2

    def body(i, carry):
        t0 = pl.multiple_of(i * POOL_TM, POOL_TM)
        slab = zp_ref[pl.ds(t0, POOL_TM + 2 * POOL_PAD), :]
        sums = _dot(band, slab)
        t = t0 + lax.broadcasted_iota(jnp.int32, (POOL_TM, cg), 0)
        cnt = (jnp.minimum(t + half, seq) - jnp.maximum(t - half, 0)).astype(F32)
        zc = zp_ref[pl.ds(t0 + POOL_PAD, POOL_TM), :].astype(F32)
        p = sums / cnt - zc
        y = _dot(p.astype(BF16), w) * ps
        o_ref[0, pl.ds(t0, POOL_TM), :] = y.astype(o_ref.dtype)
        return carry

    lax.fori_loop(0, seq // POOL_TM, body, 0)


def _pool_group(z3, w_pool, pool_scale, g, batch, seq):
    win = POOL_WINDOWS[g]
    half = win // 2
    tau = np.arange(POOL_TM)[:, None]
    kap = np.arange(POOL_TM + 2 * POOL_PAD)[None, :]
    off = kap - POOL_PAD - tau
    band = jnp.asarray(((off >= -half) & (off < half)).astype(np.float32), BF16)
    col = N_FOURIER + g
    return pl.pallas_call(
        functools.partial(_pool_kernel, win=win, seq=seq),
        grid=(batch,),
        in_specs=[pl.BlockSpec((1, seq, GROUP_W), lambda bb: (bb, 0, col)),
                  pl.BlockSpec((POOL_TM, POOL_TM + 2 * POOL_PAD), lambda bb: (0, 0)),
                  pl.BlockSpec((1, GROUP_W, GROUP_W), lambda bb: (g, 0, 0)),
                  pl.BlockSpec((1, 1, GROUP_W), lambda bb: (g, 0, 0))],
        out_specs=pl.BlockSpec((1, seq, GROUP_W), lambda bb: (bb, 0, 0)),
        out_shape=jax.ShapeDtypeStruct((batch, seq, GROUP_W), BF16),
        scratch_shapes=[pltpu.VMEM((seq + 2 * POOL_PAD, GROUP_W), BF16)],
        compiler_params=_params("parallel"),
        name=f"pool_w{win}",
    )(z3, band, w_pool, pool_scale)


def _fourier_pool_mixer(h, w_in, w_fourier, w_pool, pool_scale, w_out, batch, seq):
    t = h.shape[0]
    z = _mm(h, w_in.astype(BF16), BF16)
    a, b = _fourier_proj(z, _fourier_weights(w_fourier, seq))
    yf = _seq_dft_real(a, b, batch, seq)
    z3 = z.reshape(batch, seq, z.shape[1])
    wp = w_pool.astype(BF16)
    ps = pool_scale.reshape(N_POOL, 1, GROUP_W)
    yp = jnp.concatenate([_pool_group(z3, wp, ps, g, batch, seq) for g in range(N_POOL)], axis=-1)
    return _mm2(yf, yp.reshape(t, N_POOL * GROUP_W), w_out.astype(BF16))


def _qk_prep_kernel(x_ref, g_ref, cos_ref, sin_ref, o_ref):
    x = x_ref[...].astype(F32)
    y = x * lax.rsqrt(jnp.mean(x * x, axis=-1, keepdims=True) + LN_EPS) * g_ref[0]
    lane = lax.broadcasted_iota(jnp.int32, y.shape, 1)
    first = (lane % (HEAD // 2)) < (HEAD // 4)
    partner = jnp.where(first, pltpu.roll(y, HEAD - HEAD // 4, 1), pltpu.roll(y, HEAD // 4, 1))
    o_ref[...] = (y * cos_ref[...] + partner * sin_ref[...]).astype(o_ref.dtype)


def _rope_tables(seq):
    quarter = HEAD // 4
    inv = ROPE_THETA ** (-np.arange(quarter, dtype=np.float64) / quarter)
    t = np.arange(seq)
    ang_r = (t // GRID_COLS)[:, None] * inv[None, :]
    ang_c = (t % GRID_COLS)[:, None] * inv[None, :]
    cos = np.concatenate([np.cos(ang_r), np.cos(ang_r), np.cos(ang_c), np.cos(ang_c)], axis=1)
    sin = np.concatenate([-np.sin(ang_r), np.sin(ang_r), -np.sin(ang_c), np.sin(ang_c)], axis=1)
    return jnp.asarray(cos, F32), jnp.asarray(sin, F32)


def _qk_prep(proj, gains, seq):
    t = proj.shape[0]
    nh = gains.shape[0]
    tm = 1024
    per_b = seq // tm
    cos, sin = _rope_tables(seq)
    return pl.pallas_call(
        _qk_prep_kernel,
        grid=(t // tm, nh),
        in_specs=[pl.BlockSpec((tm, HEAD), lambda i, hh: (i, hh)),
                  pl.BlockSpec((1, 1, HEAD), lambda i, hh: (hh, 0, 0)),
                  pl.BlockSpec((tm, HEAD), lambda i, hh: (i % per_b, 0)),
                  pl.BlockSpec((tm, HEAD), lambda i, hh: (i % per_b, 0))],
        out_specs=pl.BlockSpec((tm, HEAD), lambda i, hh: (i, hh)),
        out_shape=jax.ShapeDtypeStruct((t, nh * HEAD), BF16),
        compiler_params=_params("parallel", "parallel"),
        name="qk_prep",
    )(proj, gains, cos, sin)


FLASH_TQ = 512
FLASH_TK = 512
FLASH_UNROLL = 8


def _flash_kernel(q_ref, k_ref, v_ref, o_ref):
    seq = k_ref.shape[0]
    tq = q_ref.shape[0]
    for hh in range(N_Q_HEADS // N_KV_HEADS):
        q = q_ref[:, hh * HEAD:(hh + 1) * HEAD]

        def body(c, carry):
            m, l, acc = carry
            c0 = pl.multiple_of(c * FLASH_TK, FLASH_TK)
            k = k_ref[pl.ds(c0, FLASH_TK), :]
            v = v_ref[pl.ds(c0, FLASH_TK), :]
            s = _dot_nt(q, k)
            m_new = jnp.maximum(m, jnp.max(s, axis=-1, keepdims=True))
            p = jnp.exp2(s - m_new)
            corr = jnp.exp2(m - m_new)
            l = l * corr + jnp.sum(p, axis=-1, keepdims=True)
            acc = acc * corr + _dot(p.astype(BF16), v)
            return m_new, l, acc

        init = (jnp.full((tq, 1), NEG_BIG, F32), jnp.zeros((tq, 1), F32), jnp.zeros((tq, HEAD), F32))
        m, l, acc = lax.fori_loop(0, seq // FLASH_TK, body, init, unroll=FLASH_UNROLL)
        o_ref[:, hh * HEAD:(hh + 1) * HEAD] = (acc / l).astype(o_ref.dtype)


def _gqa_attention(qk, proj, batch, seq):
    t = qk.shape[0]
    per_b = seq // FLASH_TQ
    gw = (N_Q_HEADS // N_KV_HEADS) * HEAD
    k_col = N_Q_HEADS
    v_col = N_Q_HEADS + N_KV_HEADS
    return pl.pallas_call(
        _flash_kernel,
        grid=(batch, N_KV_HEADS, per_b),
        in_specs=[pl.BlockSpec((FLASH_TQ, gw), lambda bb, kv, i: (bb * per_b + i, kv)),
                  pl.BlockSpec((seq, HEAD), lambda bb, kv, i: (bb, k_col + kv)),
                  pl.BlockSpec((seq, HEAD), lambda bb, kv, i: (bb, v_col + kv))],
        out_specs=pl.BlockSpec((FLASH_TQ, gw), lambda bb, kv, i: (bb * per_b + i, kv)),
        out_shape=jax.ShapeDtypeStruct((t, N_Q_HEADS * HEAD), BF16),
        compiler_params=_params("parallel", "parallel", "parallel"),
        name="gqa_flash",
    )(qk, qk, proj)


NA_ROW_BLOCK = 8


def _na_kernel(q_ref, k_ref, v_ref, bias_ref, o_ref, *, rows, scale):
    i = pl.program_id(2)
    span = NA_ROWS * GRID_COLS
    starts, scores = [], []
    for rr in range(NA_ROW_BLOCK):
        r = i * NA_ROW_BLOCK + rr
        rs = jnp.clip(r - NA_ROWS // 2, 0, rows - NA_ROWS)
        dr0 = rs - r + (NA_ROWS - 1)
        k0 = pl.multiple_of(rs * GRID_COLS, GRID_COLS)
        q = q_ref[rr * GRID_COLS:(rr + 1) * GRID_COLS, :]
        starts.append(k0)
        scores.append(_dot_nt(q, k_ref[pl.ds(k0, span), :]) * scale + bias_ref[0, dr0])
    probs = []
    for s in scores:
        p = jnp.exp(s - jnp.max(s, axis=-1, keepdims=True))
        probs.append((p / jnp.sum(p, axis=-1, keepdims=True)).astype(BF16))
    outs = [_dot(p, v_ref[pl.ds(k0, span), :]) for p, k0 in zip(probs, starts)]
    o_ref[...] = jnp.concatenate(outs, axis=0).astype(o_ref.dtype)


def _na_bias_table(rpb):
    cols = np.arange(GRID_COLS)
    start = np.clip(cols - NA_COLS // 2, 0, GRID_COLS - NA_COLS)
    kc = np.arange(GRID_COLS)
    valid = (kc[None, :] >= start[:, None]) & (kc[None, :] < start[:, None] + NA_COLS)
    off = np.clip(kc[None, :] - cols[:, None] + (NA_COLS - 1), 0, 2 * NA_COLS - 2)
    tab = rpb[:, :, off]
    tab = jnp.where(jnp.asarray(valid)[None, None], tab, NEG_BIG)
    per_dr0 = [jnp.concatenate([tab[:, d + j] for j in range(NA_ROWS)], axis=-1) for d in range(NA_ROWS)]
    return jnp.stack(per_dr0, axis=1).astype(F32)


def _neighbourhood_attention(proj, rpb, batch, seq):
    t = proj.shape[0]
    rows = seq // GRID_COLS
    qb = NA_ROW_BLOCK * GRID_COLS
    per_b = seq // qb
    q_col = (N_Q_HEADS + 2 * N_KV_HEADS)
    k_col = q_col + N_NA_HEADS
    v_col = k_col + N_NA_HEADS
    bias = _na_bias_table(rpb)
    return pl.pallas_call(
        functools.partial(_na_kernel, rows=rows, scale=HEAD ** -0.5),
        grid=(batch, N_NA_HEADS, per_b),
        in_specs=[pl.BlockSpec((qb, HEAD), lambda bb, hh, i: (bb * per_b + i, q_col + hh)),
                  pl.BlockSpec((seq, HEAD), lambda bb, hh, i: (bb, k_col + hh)),
                  pl.BlockSpec((seq, HEAD), lambda bb, hh, i: (bb, v_col + hh)),
                  pl.BlockSpec((1, NA_ROWS, GRID_COLS, NA_ROWS * GRID_COLS), lambda bb, hh, i: (hh, 0, 0, 0))],
        out_specs=pl.BlockSpec((qb, HEAD), lambda bb, hh, i: (bb * per_b + i, hh)),
        out_shape=jax.ShapeDtypeStruct((t, N_NA_HEADS * HEAD), BF16),
        compiler_params=_params("parallel", "parallel", "parallel"),
        name="natten",
    )(proj, proj, proj, bias)


def _attention_mixer(h, w_in, q_norm, k_norm, rpb, w_out, batch, seq):
    proj = _mm(h, w_in.astype(BF16), BF16, tn=1152)
    q_gain = HEAD ** -0.5 * math.log2(math.e)
    gains = jnp.concatenate([jnp.tile(q_norm[None] * q_gain, (N_Q_HEADS, 1)),
                             jnp.tile(k_norm[None], (N_KV_HEADS, 1))], axis=0)
    qk = _qk_prep(proj, gains.reshape(N_Q_HEADS + N_KV_HEADS, 1, HEAD).astype(F32), seq)
    yc = _gqa_attention(qk, proj, batch, seq)
    yd = _neighbourhood_attention(proj, rpb, batch, seq)
    return _mm2(yc, yd, w_out.astype(BF16))


def _row(v, i, fill):
    rows = lax.broadcasted_iota(jnp.int32, v.shape, 0)
    return jnp.max(jnp.where(rows == i, v, fill), axis=0, keepdims=True)


def _top16(s, ids, big):
    n = s.shape[1]
    slot = lax.broadcasted_iota(jnp.int32, (PEER_TOPK, n), 0)
    vals = jnp.zeros((PEER_TOPK, n), F32)
    idxs = jnp.zeros((PEER_TOPK, n), jnp.int32)
    for it in range(PEER_TOPK):
        m = jnp.max(s, axis=0, keepdims=True)
        ix = jnp.min(jnp.where(s == m, ids, big), axis=0, keepdims=True)
        s = jnp.where(ids == ix, -jnp.inf, s)
        vals = jnp.where(slot == it, m, vals)
        idxs = jnp.where(slot == it, ix, idxs)
    return vals, idxs


def _pick(table, sel):
    out = jnp.zeros_like(table)
    for r in range(PEER_TOPK):
        out = jnp.where(sel == r, _row(table, r, -1), out)
    return out


def _peer_topk_kernel(q_ref, keys_ref, e1_ref, e2_ref, g_ref):
    tm = q_ref.shape[0]
    half = q_ref.shape[1] // 2
    k = PEER_TOPK
    s1 = _dot_nt(keys_ref[0, 0], q_ref[:, :half])
    s2 = _dot_nt(keys_ref[0, 1], q_ref[:, half:])
    key_ids = lax.broadcasted_iota(jnp.int32, (PEER_KEYS, tm), 0)
    v1, i1 = _top16(s1, key_ids, PEER_KEYS)
    v2, i2 = _top16(s2, key_ids, PEER_KEYS)
    i8 = lax.broadcasted_iota(jnp.int32, (8, tm), 0)
    i16 = lax.broadcasted_iota(jnp.int32, (k, tm), 0)
    ninf = -jnp.inf
    cand = [_row(v1, 0, ninf) + v2]
    ids = [i16]
    for i in (1, 2, 3):
        cand.append(_row(v1, i, ninf) + v2[0:8])
        ids.append(i * k + i8)
    cand.append(v1[8:16] + _row(v2, 0, ninf))
    ids.append((i8 + 8) * k)
    for j in (0, 1):
        cand.append(jnp.where(i8 < 4, ninf, v1[0:8] + _row(v2, j, ninf)))
        ids.append(i8 * k + j)
    cand = jnp.concatenate(cand, axis=0)
    ids = jnp.concatenate(ids, axis=0)
    sc, ci = _top16(cand, ids, k * k)
    e1 = _pick(i1, lax.shift_right_logical(ci, 4))
    e2 = _pick(i2, lax.bitwise_and(ci, k - 1))
    ex = jnp.exp(sc - _row(sc, 0, ninf))
    gates = ex / jnp.sum(ex, axis=0, keepdims=True)
    e1_ref[0] = e1.astype(F32)
    e2_ref[0] = e2.astype(F32)
    g_ref[0] = gates


def _peer_topk(q, sub_keys):
    t = q.shape[0]
    tm = 256
    qd = q.shape[1] // PEER_HEADS
    out = jax.ShapeDtypeStruct((PEER_HEADS, PEER_TOPK, t), F32)
    spec = pl.BlockSpec((1, PEER_TOPK, tm), lambda i, hh: (hh, 0, i))
    return pl.pallas_call(
        _peer_topk_kernel,
        grid=(t // tm, PEER_HEADS),
        in_specs=[pl.BlockSpec((tm, qd), lambda i, hh: (i, hh)),
                  pl.BlockSpec((1, 2, PEER_KEYS, qd // 2), lambda i, hh: (hh, 0, 0, 0))],
        out_specs=[spec, spec, spec],
        out_shape=[out, out, out],
        compiler_params=_params("parallel", "parallel"),
        name="peer_topk",
    )(q, sub_keys)


PEER_TM = 512
PEER_TE = 1024
PEER_SUB = 512
PEER_PASSES = 2
PEER_BUILD_UNROLL = 32
PEER_ROW_PAD = 8


def _peer_experts_kernel(h_ref, e1_ref, e2_ref, gt_ref, u_ref, v_ref, o_ref, gmat, e1s, e2s, gs):
    p = pl.program_id(1)
    c = pl.program_id(2)
    tm = h_ref.shape[0]
    te = v_ref.shape[0]
    slots = PEER_HEADS * PEER_TOPK
    pitch = gmat.shape[0] // tm
    kpp = pitch - PEER_ROW_PAD

    @pl.when((p == 0) & (c == 0))
    def _():
        o_ref[...] = jnp.zeros_like(o_ref)
        e1s[...] = e1_ref[...].reshape(slots, tm).T
        e2s[...] = e2_ref[...].reshape(slots, tm).T
        gs[...] = gt_ref[...].reshape(slots, tm).T

    @pl.when(c == 0)
    def _():
        key1 = (lax.broadcasted_iota(jnp.int32, (kpp, slots), 0) + p * kpp).astype(F32)
        key2 = lax.broadcasted_iota(jnp.int32, (PEER_KEYS, slots), 0).astype(F32)

        def body(t, carry):
            r1 = e1s[pl.ds(t, 1), :]
            r2 = e2s[pl.ds(t, 1), :]
            rg = gs[pl.ds(t, 1), :]
            a = jnp.where(key1 == r1, 1.0, 0.0).astype(BF16)
            b = jnp.where(key2 == r2, rg, 0.0).astype(BF16)
            gmat[pl.ds(pl.multiple_of(t * pitch, PEER_ROW_PAD), kpp), :] = _dot_nt(a, b)
            return carry

        lax.fori_loop(0, tm, body, 0, unroll=PEER_BUILD_UNROLL)

    per_step = te // PEER_KEYS
    per_sub = PEER_SUB // PEER_KEYS
    subs = [slice(sb * PEER_SUB, (sb + 1) * PEER_SUB) for sb in range(te // PEER_SUB)]
    pre = [_dot_nt(h_ref[...], u_ref[rows, :]) for rows in subs]
    gated = []
    for sb, a in enumerate(pre):
        act = 0.5 * a * (1.0 + lax.erf(a * (1.0 / math.sqrt(2.0))))
        g = jnp.concatenate([gmat[pl.ds(c * per_step + sb * per_sub + j, tm, stride=pitch), :]
                             for j in range(per_sub)], axis=1)
        gated.append((act * g).astype(BF16))
    for rows, pg in zip(subs, gated):
        o_ref[...] += _dot(pg, v_ref[rows, :])


def _peer_experts(h, e1, e2, gates, u, v):
    t, d = h.shape
    n_exp = v.shape[0]
    tm = min(PEER_TM, t)
    te = PEER_TE
    kpp = PEER_KEYS // PEER_PASSES
    cpp = n_exp // PEER_PASSES // te
    slots = PEER_HEADS * PEER_TOPK
    sel = pl.BlockSpec((PEER_HEADS, PEER_TOPK, tm), lambda i, p, c: (0, 0, i))
    return pl.pallas_call(
        _peer_experts_kernel,
        grid=(t // tm, PEER_PASSES, cpp),
        in_specs=[pl.BlockSpec((tm, d), lambda i, p, c: (i, 0)),
                  sel, sel, sel,
                  pl.BlockSpec((te, d), lambda i, p, c: (p * cpp + c, 0)),
                  pl.BlockSpec((te, d), lambda i, p, c: (p * cpp + c, 0))],
        out_specs=pl.BlockSpec((tm, d), lambda i, p, c: (i, 0)),
        out_shape=jax.ShapeDtypeStruct((t, d), F32),
        scratch_shapes=[pltpu.VMEM((tm * (kpp + PEER_ROW_PAD), PEER_KEYS), F32)]
        + [pltpu.VMEM((tm, slots), F32)] * 3,
        compiler_params=_params("parallel", "arbitrary", "arbitrary"),
        name="peer_experts",
    )(h, e1, e2, gates, u, v)


def _peer(h, w_q, sub_keys, u, v):
    q = _mm(h, w_q.astype(BF16), BF16)
    e1, e2, gates = _peer_topk(q, sub_keys.astype(BF16))
    return _peer_experts(h, e1, e2, gates, u.astype(BF16), v.astype(BF16))


def kernel(x, c, ada_w, ada_b, ln_g, ln_b, fp_w_in, fp_w_fourier, fp_w_pool, fp_pool_scale, fp_w_out,
           at_w_in, at_q_norm, at_k_norm, at_rpb, at_w_out, peer_w_q, peer_sub_keys, peer_u, peer_v):
    batch, seq, d = x.shape
    depth = ada_w.shape[0]
    alpha = (2.0 * depth) ** 0.25
    xt = x.reshape(batch * seq, d)
    ada = _adaln_all(c, ada_w, ada_b)[:, :batch]
    shift = ada[:, :, None, :d]
    scale = ada[:, :, None, d:2 * d]
    gate = ada[:, :, None, 2 * d:]
    h = _modulate(xt, shift[0], scale[0], seq)
    for n in range(2 * depth):
        l, sub = divmod(n, 2)
        i = l // 2
        if sub == 1:
            y = _peer(h, peer_w_q[l], peer_sub_keys[l], peer_u[l], peer_v[l])
        elif l % 2 == 0:
            y = _fourier_pool_mixer(h, fp_w_in[i], fp_w_fourier[i], fp_w_pool[i], fp_pool_scale[i],
                                    fp_w_out[i], batch, seq)
        else:
            y = _attention_mixer(h, at_w_in[i], at_q_norm[i], at_k_norm[i], at_rpb[i], at_w_out[i],
                                 batch, seq)
        if n + 1 < 2 * depth:
            xt, h = _postnorm_modulate(xt, y, gate[n], ln_g[l, sub], ln_b[l, sub],
                                       shift[n + 1], scale[n + 1], seq, alpha)
        else:
            xt = _postnorm(xt, y, gate[n], ln_g[l, sub], ln_b[l, sub], seq, alpha)
    return xt.reshape(batch, seq, d)
```

```python
import functools
import math

import numpy as np
import jax
import jax.numpy as jnp
from jax import lax
from jax.experimental import pallas as pl
from jax.experimental.pallas import tpu as pltpu

F32 = jnp.float32
BF16 = jnp.bfloat16

GRID_COLS = 64
HEAD = 128
GROUP_W = 256
N_FOURIER = 4
N_POOL = 4
POOL_WINDOWS = (2, 4, 8, 16)
N_Q_HEADS = 8
N_KV_HEADS = 2
N_NA_HEADS = 8
NA_ROWS = 8
NA_COLS = 16
ROPE_THETA = 10000.0
PEER_HEADS = 8
PEER_KEYS = 128
PEER_TOPK = 16
LN_EPS = 1e-6
FFT_N1 = 128
FFT_K1_BLOCK = 8

VMEM_LIMIT_BYTES = 56 * 1024 * 1024
LANES = 128
NEG_BIG = -1e30


def _params(*sem):
    return pltpu.CompilerParams(dimension_semantics=sem, vmem_limit_bytes=VMEM_LIMIT_BYTES)


def _dot(a, b):
    return jnp.dot(a, b, preferred_element_type=F32)


def _dot_nt(a, b):
    return lax.dot_general(a, b, (((1,), (1,)), ((), ())), preferred_element_type=F32)


def _layer_norm(x):
    mu = jnp.mean(x, axis=-1, keepdims=True)
    xc = x - mu
    var = jnp.mean(xc * xc, axis=-1, keepdims=True)
    return xc * lax.rsqrt(var + LN_EPS)


def _adaln_kernel(c_ref, w_ref, b_ref, o_ref):
    c = c_ref[...].astype(BF16)
    w = w_ref[0].astype(BF16)
    o_ref[0] = _dot(c, w) + b_ref[0]


def _adaln_all(c, ada_w, ada_b):
    depth, two, d, d3 = ada_w.shape
    n = depth * two
    b = c.shape[0]
    c_pad = jnp.zeros((8, d), F32).at[:b].set(c)
    w = ada_w.reshape(n, d, d3)
    bias = ada_b.reshape(n, 1, d3)
    tn = 768
    return pl.pallas_call(
        _adaln_kernel,
        grid=(n, d3 // tn),
        in_specs=[pl.BlockSpec((8, d), lambda i, j: (0, 0)),
                  pl.BlockSpec((1, d, tn), lambda i, j: (i, 0, j)),
                  pl.BlockSpec((1, 1, tn), lambda i, j: (i, 0, j))],
        out_specs=pl.BlockSpec((1, 8, tn), lambda i, j: (i, 0, j)),
        out_shape=jax.ShapeDtypeStruct((n, 8, d3), F32),
        compiler_params=_params("parallel", "parallel"),
        name="adaln",
    )(c_pad, w, bias)


def _modulate_kernel(x_ref, sh_ref, sc_ref, o_ref):
    y = _layer_norm(x_ref[...])
    o_ref[...] = (y * (1.0 + sc_ref[0]) + sh_ref[0]).astype(o_ref.dtype)


def _modulate(x, shift, scale, seq):
    t, d = x.shape
    tm = 512
    per_b = seq // tm
    return pl.pallas_call(
        _modulate_kernel,
        grid=(t // tm,),
        in_specs=[pl.BlockSpec((tm, d), lambda i: (i, 0)),
                  pl.BlockSpec((1, 1, d), lambda i: (i // per_b, 0, 0)),
                  pl.BlockSpec((1, 1, d), lambda i: (i // per_b, 0, 0))],
        out_specs=pl.BlockSpec((tm, d), lambda i: (i, 0)),
        out_shape=jax.ShapeDtypeStruct((t, d), BF16),
        compiler_params=_params("parallel"),
        name="modulate",
    )(x, shift, scale)


def _postnorm_kernel(x_ref, y_ref, gate_ref, g_ref, b_ref, o_ref, *, alpha):
    u = alpha * x_ref[...] + (1.0 + gate_ref[0]) * y_ref[...]
    o_ref[...] = _layer_norm(u) * g_ref[...] + b_ref[...]


def _postnorm(x, y, gate, g, b, seq, alpha):
    t, d = x.shape
    tm = 512
    per_b = seq // tm
    return pl.pallas_call(
        functools.partial(_postnorm_kernel, alpha=alpha),
        grid=(t // tm,),
        in_specs=[pl.BlockSpec((tm, d), lambda i: (i, 0)),
                  pl.BlockSpec((tm, d), lambda i: (i, 0)),
                  pl.BlockSpec((1, 1, d), lambda i: (i // per_b, 0, 0)),
                  pl.BlockSpec((1, d), lambda i: (0, 0)),
                  pl.BlockSpec((1, d), lambda i: (0, 0))],
        out_specs=pl.BlockSpec((tm, d), lambda i: (i, 0)),
        out_shape=jax.ShapeDtypeStruct((t, d), F32),
        compiler_params=_params("parallel"),
        name="postnorm",
    )(x, y, gate, g.reshape(1, d), b.reshape(1, d))


def _postnorm_modulate_kernel(x_ref, y_ref, gate_ref, g_ref, b_ref, sh_ref, sc_ref, o_ref, h_ref, *, alpha):
    u = alpha * x_ref[...] + (1.0 + gate_ref[0]) * y_ref[...]
    xn = _layer_norm(u) * g_ref[...] + b_ref[...]
    o_ref[...] = xn
    h_ref[...] = (_layer_norm(xn) * (1.0 + sc_ref[0]) + sh_ref[0]).astype(h_ref.dtype)


def _postnorm_modulate(x, y, gate, g, b, shift, scale, seq, alpha):
    t, d = x.shape
    tm = 512
    per_b = seq // tm
    row = pl.BlockSpec((tm, d), lambda i: (i, 0))
    per_batch = pl.BlockSpec((1, 1, d), lambda i: (i // per_b, 0, 0))
    vec = pl.BlockSpec((1, d), lambda i: (0, 0))
    return pl.pallas_call(
        functools.partial(_postnorm_modulate_kernel, alpha=alpha),
        grid=(t // tm,),
        in_specs=[row, row, per_batch, vec, vec, per_batch, per_batch],
        out_specs=[row, row],
        out_shape=[jax.ShapeDtypeStruct((t, d), F32), jax.ShapeDtypeStruct((t, d), BF16)],
        compiler_params=_params("parallel"),
        name="postnorm_modulate",
    )(x, y, gate, g.reshape(1, d), b.reshape(1, d), shift, scale)


def _mm_kernel(a_ref, b_ref, o_ref):
    o_ref[...] = _dot(a_ref[...], b_ref[...]).astype(o_ref.dtype)


def _mm(a, w, out_dtype, tm=512, tn=1024):
    m, k = a.shape
    n = w.shape[1]
    tm = min(tm, m)
    tn = min(tn, n)
    return pl.pallas_call(
        _mm_kernel,
        grid=(n // tn, m // tm),
        in_specs=[pl.BlockSpec((tm, k), lambda j, i: (i, 0)),
                  pl.BlockSpec((k, tn), lambda j, i: (0, j))],
        out_specs=pl.BlockSpec((tm, tn), lambda j, i: (i, j)),
        out_shape=jax.ShapeDtypeStruct((m, n), out_dtype),
        compiler_params=_params("parallel", "parallel"),
        name="mm",
    )(a, w)


def _mm2_kernel(a1_ref, a2_ref, w1_ref, w2_ref, o_ref):
    acc = _dot(a1_ref[...].astype(BF16), w1_ref[...])
    acc = acc + _dot(a2_ref[...].astype(BF16), w2_ref[...])
    o_ref[...] = acc


def _mm2(a1, a2, w, tm=512, tn=1024):
    m, k = a1.shape
    n = w.shape[1]
    tm = min(tm, m)
    tn = min(tn, n)
    return pl.pallas_call(
        _mm2_kernel,
        grid=(n // tn, m // tm),
        in_specs=[pl.BlockSpec((tm, k), lambda j, i: (i, 0)),
                  pl.BlockSpec((tm, k), lambda j, i: (i, 0)),
                  pl.BlockSpec((k, tn), lambda j, i: (0, j)),
                  pl.BlockSpec((k, tn), lambda j, i: (1, j))],
        out_specs=pl.BlockSpec((tm, tn), lambda j, i: (i, j)),
        out_shape=jax.ShapeDtypeStruct((m, n), F32),
        compiler_params=_params("parallel", "parallel"),
        name="mm2",
    )(a1, a2, w, w)


def _fourier_proj_kernel(z_ref, w_ref, a_ref, b_ref):
    r = _dot(z_ref[...], w_ref[0])
    a_ref[...] = r[:, :GROUP_W].astype(a_ref.dtype)
    b_ref[...] = r[:, GROUP_W:].astype(b_ref.dtype)


def _fourier_proj(z, w_ab):
    t = z.shape[0]
    tm = 1024
    wid = N_FOURIER * GROUP_W
    return pl.pallas_call(
        _fourier_proj_kernel,
        grid=(N_FOURIER, t // tm),
        in_specs=[pl.BlockSpec((tm, GROUP_W), lambda g, i: (i, g)),
                  pl.BlockSpec((1, GROUP_W, 2 * GROUP_W), lambda g, i: (g, 0, 0))],
        out_specs=[pl.BlockSpec((tm, GROUP_W), lambda g, i: (i, g)),
                   pl.BlockSpec((tm, GROUP_W), lambda g, i: (i, g))],
        out_shape=[jax.ShapeDtypeStruct((t, wid), BF16), jax.ShapeDtypeStruct((t, wid), BF16)],
        compiler_params=_params("parallel", "parallel"),
        name="fourier_proj",
    )(z, w_ab)


def _fft1_kernel(a_ref, b_ref, c_ref, s_ref, twc_ref, tws_ref, tr_ref, ti_ref):
    a = a_ref[0]
    b = b_ref[0]
    c = c_ref[...]
    s = s_ref[...]
    tr = _dot(c, a) + _dot(s, b)
    ti = _dot(c, b) - _dot(s, a)
    reps = a.shape[1] // LANES
    wc = jnp.concatenate([twc_ref[0]] * reps, axis=1)
    ws = jnp.concatenate([tws_ref[0]] * reps, axis=1)
    tr_ref[0] = (tr * wc + ti * ws).astype(tr_ref.dtype)
    ti_ref[0] = (ti * wc - tr * ws).astype(ti_ref.dtype)


def _fft2_kernel(tr_ref, ti_ref, lr_ref, li_ref, o_ref):
    kb, n2, cols = tr_ref.shape[1:]
    tr = tr_ref[0].reshape(kb * n2, cols)
    ti = ti_ref[0].reshape(kb * n2, cols)
    res = _dot(lr_ref[...], tr) + _dot(li_ref[...], ti)
    o_ref[0] = res.reshape(n2, kb, cols)


def _seq_dft_real(a, b, batch, seq):
    wid = a.shape[1]
    n1 = FFT_N1
    n2 = seq // n1
    kb = FFT_K1_BLOCK
    k = np.arange(n1)
    ang1 = 2.0 * np.pi * np.outer(k, k) / n1
    c1 = jnp.asarray(np.cos(ang1), BF16)
    s1 = jnp.asarray(np.sin(ang1), BF16)
    angt = 2.0 * np.pi * np.outer(np.arange(n2), k) / seq
    twc = jnp.asarray(np.broadcast_to(np.cos(angt)[:, :, None], (n2, n1, LANES)), F32)
    tws = jnp.asarray(np.broadcast_to(np.sin(angt)[:, :, None], (n2, n1, LANES)), F32)
    a3 = a.reshape(batch, n1, n2 * wid)
    b3 = b.reshape(batch, n1, n2 * wid)
    tr, ti = pl.pallas_call(
        _fft1_kernel,
        grid=(batch, n2),
        in_specs=[pl.BlockSpec((1, n1, wid), lambda bb, j: (bb, 0, j)),
                  pl.BlockSpec((1, n1, wid), lambda bb, j: (bb, 0, j)),
                  pl.BlockSpec((n1, n1), lambda bb, j: (0, 0)),
                  pl.BlockSpec((n1, n1), lambda bb, j: (0, 0)),
                  pl.BlockSpec((1, n1, LANES), lambda bb, j: (j, 0, 0)),
                  pl.BlockSpec((1, n1, LANES), lambda bb, j: (j, 0, 0))],
        out_specs=[pl.BlockSpec((1, n1, wid), lambda bb, j: (bb, 0, j)),
                   pl.BlockSpec((1, n1, wid), lambda bb, j: (bb, 0, j))],
        out_shape=[jax.ShapeDtypeStruct((batch, n1, n2 * wid), BF16)] * 2,
        compiler_params=_params("parallel", "parallel"),
        name="fft_stage1",
    )(a3, b3, c1, s1, twc, tws)
    k2 = np.arange(n2)
    ang2 = 2.0 * np.pi * np.outer(k2, k2) / n2
    eye = np.eye(kb)
    lr = np.einsum('kn,jJ->kjJn', np.cos(ang2), eye).reshape(n2 * kb, kb * n2)
    li = np.einsum('kn,jJ->kjJn', np.sin(ang2), eye).reshape(n2 * kb, kb * n2)
    tr4 = tr.reshape(batch, n1, n2, wid)
    ti4 = ti.reshape(batch, n1, n2, wid)
    out = pl.pallas_call(
        _fft2_kernel,
        grid=(batch, n1 // kb),
        in_specs=[pl.BlockSpec((1, kb, n2, wid), lambda bb, j: (bb, j, 0, 0)),
                  pl.BlockSpec((1, kb, n2, wid), lambda bb, j: (bb, j, 0, 0)),
                  pl.BlockSpec((n2 * kb, kb * n2), lambda bb, j: (0, 0)),
                  pl.BlockSpec((n2 * kb, kb * n2), lambda bb, j: (0, 0))],
        out_specs=pl.BlockSpec((1, n2, kb, wid), lambda bb, j: (bb, 0, j, 0)),
        out_shape=jax.ShapeDtypeStruct((batch, n2, n1, wid), F32),
        compiler_params=_params("parallel", "parallel"),
        name="fft_stage2",
    )(tr4, ti4, jnp.asarray(lr, BF16), jnp.asarray(li, BF16))
    return out.reshape(batch * seq, wid)


def _fourier_weights(w_fourier, seq):
    g, cg, _ = w_fourier.shape
    k = np.arange(cg)
    ang = 2.0 * np.pi * np.outer(k, k) / cg
    norm = 1.0 / math.sqrt(seq * cg)
    cs = jnp.asarray(np.concatenate([np.cos(ang), -np.sin(ang)], axis=0) * norm, BF16)
    wf = jnp.transpose(w_fourier, (1, 0, 2)).reshape(cg, g * cg).astype(BF16)
    r = _mm(cs, wf, F32)
    r = r.reshape(2, cg, g, cg)
    return jnp.concatenate([r[0], r[1]], axis=-1).transpose(1, 0, 2).astype(BF16)


POOL_TM = 256
POOL_PAD = 128


def _pool_kernel(z_ref, band_ref, w_ref, ps_ref, o_ref, zp_ref, *, win, seq):
    cg = z_ref.shape[2]
    zp_ref[0:POOL_PAD, :] = jnp.zeros((POOL_PAD, cg), BF16)
    zp_ref[seq + POOL_PAD:seq + 2 * POOL_PAD, :] = jnp.zeros((POOL_PAD, cg), BF16)
    zp_ref[POOL_PAD:seq + POOL_PAD, :] = z_ref[0]
    band = band_ref[...]
    w = w_ref[0]
    ps = ps_ref[0]
    half = win // 2

    def body(i, carry):
        t0 = pl.multiple_of(i * POOL_TM, POOL_TM)
        slab = zp_ref[pl.ds(t0, POOL_TM + 2 * POOL_PAD), :]
        sums = _dot(band, slab)
        t = t0 + lax.broadcasted_iota(jnp.int32, (POOL_TM, cg), 0)
        cnt = (jnp.minimum(t + half, seq) - jnp.maximum(t - half, 0)).astype(F32)
        zc = zp_ref[pl.ds(t0 + POOL_PAD, POOL_TM), :].astype(F32)
        p = sums / cnt - zc
        y = _dot(p.astype(BF16), w) * ps
        o_ref[0, pl.ds(t0, POOL_TM), :] = y.astype(o_ref.dtype)
        return carry

    lax.fori_loop(0, seq // POOL_TM, body, 0)


def _pool_group(z3, w_pool, pool_scale, g, batch, seq):
    win = POOL_WINDOWS[g]
    half = win // 2
    tau = np.arange(POOL_TM)[:, None]
    kap = np.arange(POOL_TM + 2 * POOL_PAD)[None, :]
    off = kap - POOL_PAD - tau
    band = jnp.asarray(((off >= -half) & (off < half)).astype(np.float32), BF16)
    col = N_FOURIER + g
    return pl.pallas_call(
        functools.partial(_pool_kernel, win=win, seq=seq),
        grid=(batch,),
        in_specs=[pl.BlockSpec((1, seq, GROUP_W), lambda bb: (bb, 0, col)),
                  pl.BlockSpec((POOL_TM, POOL_TM + 2 * POOL_PAD), lambda bb: (0, 0)),
                  pl.BlockSpec((1, GROUP_W, GROUP_W), lambda bb: (g, 0, 0)),
                  pl.BlockSpec((1, 1, GROUP_W), lambda bb: (g, 0, 0))],
        out_specs=pl.BlockSpec((1, seq, GROUP_W), lambda bb: (bb, 0, 0)),
        out_shape=jax.ShapeDtypeStruct((batch, seq, GROUP_W), BF16),
        scratch_shapes=[pltpu.VMEM((seq + 2 * POOL_PAD, GROUP_W), BF16)],
        compiler_params=_params("parallel"),
        name=f"pool_w{win}",
    )(z3, band, w_pool, pool_scale)


def _fourier_pool_mixer(h, w_in, w_fourier, w_pool, pool_scale, w_out, batch, seq):
    t = h.shape[0]
    z = _mm(h, w_in.astype(BF16), BF16)
    a, b = _fourier_proj(z, _fourier_weights(w_fourier, seq))
    yf = _seq_dft_real(a, b, batch, seq)
    z3 = z.reshape(batch, seq, z.shape[1])
    wp = w_pool.astype(BF16)
    ps = pool_scale.reshape(N_POOL, 1, GROUP_W)
    yp = jnp.concatenate([_pool_group(z3, wp, ps, g, batch, seq) for g in range(N_POOL)], axis=-1)
    return _mm2(yf, yp.reshape(t, N_POOL * GROUP_W), w_out.astype(BF16))


def _qk_prep_kernel(x_ref, g_ref, cos_ref, sin_ref, o_ref):
    x = x_ref[...].astype(F32)
    y = x * lax.rsqrt(jnp.mean(x * x, axis=-1, keepdims=True) + LN_EPS) * g_ref[0]
    lane = lax.broadcasted_iota(jnp.int32, y.shape, 1)
    first = (lane % (HEAD // 2)) < (HEAD // 4)
    partner = jnp.where(first, pltpu.roll(y, HEAD - HEAD // 4, 1), pltpu.roll(y, HEAD // 4, 1))
    o_ref[...] = (y * cos_ref[...] + partner * sin_ref[...]).astype(o_ref.dtype)


def _rope_tables(seq):
    quarter = HEAD // 4
    inv = ROPE_THETA ** (-np.arange(quarter, dtype=np.float64) / quarter)
    t = np.arange(seq)
    ang_r = (t // GRID_COLS)[:, None] * inv[None, :]
    ang_c = (t % GRID_COLS)[:, None] * inv[None, :]
    cos = np.concatenate([np.cos(ang_r), np.cos(ang_r), np.cos(ang_c), np.cos(ang_c)], axis=1)
    sin = np.concatenate([-np.sin(ang_r), np.sin(ang_r), -np.sin(ang_c), np.sin(ang_c)], axis=1)
    return jnp.asarray(cos, F32), jnp.asarray(sin, F32)


def _qk_prep(proj, gains, seq):
    t = proj.shape[0]
    nh = gains.shape[0]
    tm = 1024
    per_b = seq // tm
    cos, sin = _rope_tables(seq)
    return pl.pallas_call(
        _qk_prep_kernel,
        grid=(t // tm, nh),
        in_specs=[pl.BlockSpec((tm, HEAD), lambda i, hh: (i, hh)),
                  pl.BlockSpec((1, 1, HEAD), lambda i, hh: (hh, 0, 0)),
                  pl.BlockSpec((tm, HEAD), lambda i, hh: (i % per_b, 0)),
                  pl.BlockSpec((tm, HEAD), lambda i, hh: (i % per_b, 0))],
        out_specs=pl.BlockSpec((tm, HEAD), lambda i, hh: (i, hh)),
        out_shape=jax.ShapeDtypeStruct((t, nh * HEAD), BF16),
        compiler_params=_params("parallel", "parallel"),
        name="qk_prep",
    )(proj, gains, cos, sin)


FLASH_TQ = 512
FLASH_TK = 1024
FLASH_UNROLL = 4


def _flash_kernel(q_ref, k_ref, v_ref, o_ref):
    seq = k_ref.shape[0]
    tq = q_ref.shape[0]
    for hh in range(N_Q_HEADS // N_KV_HEADS):
        q = q_ref[:, hh * HEAD:(hh + 1) * HEAD]

        def body(c, carry):
            m, l, acc = carry
            c0 = pl.multiple_of(c * FLASH_TK, FLASH_TK)
            k = k_ref[pl.ds(c0, FLASH_TK), :]
            v = v_ref[pl.ds(c0, FLASH_TK), :]
            s = _dot_nt(q, k)
            m_new = jnp.maximum(m, jnp.max(s, axis=-1, keepdims=True))
            p = jnp.exp2(s - m_new)
            corr = jnp.exp2(m - m_new)
            l = l * corr + jnp.sum(p, axis=-1, keepdims=True)
            acc = acc * corr + _dot(p.astype(BF16), v)
            return m_new, l, acc

        init = (jnp.full((tq, 1), NEG_BIG, F32), jnp.zeros((tq, 1), F32), jnp.zeros((tq, HEAD), F32))
        m, l, acc = lax.fori_loop(0, seq // FLASH_TK, body, init, unroll=FLASH_UNROLL)
        o_ref[:, hh * HEAD:(hh + 1) * HEAD] = (acc / l).astype(o_ref.dtype)


def _gqa_attention(qk, proj, batch, seq):
    t = qk.shape[0]
    per_b = seq // FLASH_TQ
    gw = (N_Q_HEADS // N_KV_HEADS) * HEAD
    k_col = N_Q_HEADS
    v_col = N_Q_HEADS + N_KV_HEADS
    return pl.pallas_call(
        _flash_kernel,
        grid=(batch, N_KV_HEADS, per_b),
        in_specs=[pl.BlockSpec((FLASH_TQ, gw), lambda bb, kv, i: (bb * per_b + i, kv)),
                  pl.BlockSpec((seq, HEAD), lambda bb, kv, i: (bb, k_col + kv)),
                  pl.BlockSpec((seq, HEAD), lambda bb, kv, i: (bb, v_col + kv))],
        out_specs=pl.BlockSpec((FLASH_TQ, gw), lambda bb, kv, i: (bb * per_b + i, kv)),
        out_shape=jax.ShapeDtypeStruct((t, N_Q_HEADS * HEAD), BF16),
        compiler_params=_params("parallel", "parallel", "parallel"),
        name="gqa_flash",
    )(qk, qk, proj)


NA_ROW_BLOCK = 8


def _na_kernel(q_ref, k_ref, v_ref, bias_ref, o_ref, *, rows, scale):
    i = pl.program_id(2)
    span = NA_ROWS * GRID_COLS
    starts, scores = [], []
    for rr in range(NA_ROW_BLOCK):
        r = i * NA_ROW_BLOCK + rr
        rs = jnp.clip(r - NA_ROWS // 2, 0, rows - NA_ROWS)
        dr0 = rs - r + (NA_ROWS - 1)
        k0 = pl.multiple_of(rs * GRID_COLS, GRID_COLS)
        q = q_ref[rr * GRID_COLS:(rr + 1) * GRID_COLS, :]
        starts.append(k0)
        scores.append(_dot_nt(q, k_ref[pl.ds(k0, span), :]) * scale + bias_ref[0, dr0])
    probs = []
    for s in scores:
        p = jnp.exp(s - jnp.max(s, axis=-1, keepdims=True))
        probs.append((p / jnp.sum(p, axis=-1, keepdims=True)).astype(BF16))
    outs = [_dot(p, v_ref[pl.ds(k0, span), :]) for p, k0 in zip(probs, starts)]
    o_ref[...] = jnp.concatenate(outs, axis=0).astype(o_ref.dtype)


def _na_bias_table(rpb):
    cols = np.arange(GRID_COLS)
    start = np.clip(cols - NA_COLS // 2, 0, GRID_COLS - NA_COLS)
    kc = np.arange(GRID_COLS)
    valid = (kc[None, :] >= start[:, None]) & (kc[None, :] < start[:, None] + NA_COLS)
    off = np.clip(kc[None, :] - cols[:, None] + (NA_COLS - 1), 0, 2 * NA_COLS - 2)
    tab = rpb[:, :, off]
    tab = jnp.where(jnp.asarray(valid)[None, None], tab, NEG_BIG)
    per_dr0 = [jnp.concatenate([tab[:, d + j] for j in range(NA_ROWS)], axis=-1) for d in range(NA_ROWS)]
    return jnp.stack(per_dr0, axis=1).astype(F32)


def _neighbourhood_attention(proj, rpb, batch, seq):
    t = proj.shape[0]
    rows = seq // GRID_COLS
    qb = NA_ROW_BLOCK * GRID_COLS
    per_b = seq // qb
    q_col = (N_Q_HEADS + 2 * N_KV_HEADS)
    k_col = q_col + N_NA_HEADS
    v_col = k_col + N_NA_HEADS
    bias = _na_bias_table(rpb)
    return pl.pallas_call(
        functools.partial(_na_kernel, rows=rows, scale=HEAD ** -0.5),
        grid=(batch, N_NA_HEADS, per_b),
        in_specs=[pl.BlockSpec((qb, HEAD), lambda bb, hh, i: (bb * per_b + i, q_col + hh)),
                  pl.BlockSpec((seq, HEAD), lambda bb, hh, i: (bb, k_col + hh)),
                  pl.BlockSpec((seq, HEAD), lambda bb, hh, i: (bb, v_col + hh)),
                  pl.BlockSpec((1, NA_ROWS, GRID_COLS, NA_ROWS * GRID_COLS), lambda bb, hh, i: (hh, 0, 0, 0))],
        out_specs=pl.BlockSpec((qb, HEAD), lambda bb, hh, i: (bb * per_b + i, hh)),
        out_shape=jax.ShapeDtypeStruct((t, N_NA_HEADS * HEAD), BF16),
        compiler_params=_params("parallel", "parallel", "parallel"),
        name="natten",
    )(proj, proj, proj, bias)


def _attention_mixer(h, w_in, q_norm, k_norm, rpb, w_out, batch, seq):
    proj = _mm(h, w_in.astype(BF16), BF16, tn=1152)
    q_gain = HEAD ** -0.5 * math.log2(math.e)
    gains = jnp.concatenate([jnp.tile(q_norm[None] * q_gain, (N_Q_HEADS, 1)),
                             jnp.tile(k_norm[None], (N_KV_HEADS, 1))], axis=0)
    qk = _qk_prep(proj, gains.reshape(N_Q_HEADS + N_KV_HEADS, 1, HEAD).astype(F32), seq)
    yc = _gqa_attention(qk, proj, batch, seq)
    yd = _neighbourhood_attention(proj, rpb, batch, seq)
    return _mm2(yc, yd, w_out.astype(BF16))


def _row(v, i, fill):
    rows = lax.broadcasted_iota(jnp.int32, v.shape, 0)
    return jnp.max(jnp.where(rows == i, v, fill), axis=0, keepdims=True)


def _top16(s, ids, big):
    rows, n = s.shape
    half = rows // 2
    id_a, id_b = ids
    by_id = jnp.where(id_a < id_b, 1.0, 0.0)
    first = jnp.where(s[:half] > s[half:], 1.0, jnp.where(s[:half] < s[half:], 0.0, by_id)) > 0.5
    top = jnp.where(first, s[:half], s[half:])
    low = jnp.where(first, s[half:], s[:half])
    top_id = jnp.where(first, id_a, id_b)
    low_id = jnp.where(first, id_b, id_a)
    slot = lax.broadcasted_iota(jnp.int32, (PEER_TOPK, n), 0)
    vals = jnp.zeros((PEER_TOPK, n), F32)
    idxs = jnp.zeros((PEER_TOPK, n), F32)
    for it in range(PEER_TOPK):
        m = jnp.max(top, axis=0, keepdims=True)
        ix = jnp.min(jnp.where(top == m, top_id, big), axis=0, keepdims=True)
        hit = top_id == ix
        top = jnp.where(hit, low, top)
        top_id = jnp.where(hit, low_id, top_id)
        low = jnp.where(hit, -jnp.inf, low)
        vals = jnp.where(slot == it, m, vals)
        idxs = jnp.where(slot == it, ix, idxs)
    return vals, idxs


def _pick(table, sel):
    out = jnp.zeros_like(table)
    for r in range(PEER_TOPK):
        out = jnp.where(sel == r, _row(table, r, -1.0), out)
    return out


def _peer_topk_kernel(q_ref, keys_ref, e1_ref, e2_ref, g_ref):
    tm = q_ref.shape[0]
    half = q_ref.shape[1] // 2
    k = PEER_TOPK
    s1 = _dot_nt(keys_ref[0, 0], q_ref[:, :half])
    s2 = _dot_nt(keys_ref[0, 1], q_ref[:, half:])
    key_lo = lax.broadcasted_iota(jnp.int32, (PEER_KEYS // 2, tm), 0).astype(F32)
    key_ids = (key_lo, key_lo + float(PEER_KEYS // 2))
    v1, i1 = _top16(s1, key_ids, float(PEER_KEYS))
    v2, i2 = _top16(s2, key_ids, float(PEER_KEYS))
    i8 = lax.broadcasted_iota(jnp.int32, (8, tm), 0)
    i16 = lax.broadcasted_iota(jnp.int32, (k, tm), 0)
    ninf = -jnp.inf
    cand = [_row(v1, 0, ninf) + v2]
    ids = [i16]
    for i in (1, 2, 3):
        cand.append(_row(v1, i, ninf) + v2[0:8])
        ids.append(i * k + i8)
    cand.append(v1[8:16] + _row(v2, 0, ninf))
    ids.append((i8 + 8) * k)
    for j in (0, 1):
        cand.append(jnp.where(i8 < 4, ninf, v1[0:8] + _row(v2, j, ninf)))
        ids.append(i8 * k + j)
    cand = jnp.concatenate(cand, axis=0)
    ids = (jnp.concatenate(ids[:3], axis=0).astype(F32), jnp.concatenate(ids[3:], axis=0).astype(F32))
    sc, cf = _top16(cand, ids, float(k * k))
    ci = cf.astype(jnp.int32)
    ex = jnp.exp(sc - _row(sc, 0, ninf))
    e1_ref[0] = _pick(i1, lax.shift_right_logical(ci, 4))
    e2_ref[0] = _pick(i2, lax.bitwise_and(ci, k - 1))
    g_ref[0] = ex / jnp.sum(ex, axis=0, keepdims=True)


def _peer_topk(q, sub_keys):
    t = q.shape[0]
    tm = 256
    qd = q.shape[1] // PEER_HEADS
    out = jax.ShapeDtypeStruct((PEER_HEADS, PEER_TOPK, t), F32)
    spec = pl.BlockSpec((1, PEER_TOPK, tm), lambda i, hh: (hh, 0, i))
    return pl.pallas_call(
        _peer_topk_kernel,
        grid=(t // tm, PEER_HEADS),
        in_specs=[pl.BlockSpec((tm, qd), lambda i, hh: (i, hh)),
                  pl.BlockSpec((1, 2, PEER_KEYS, qd // 2), lambda i, hh: (hh, 0, 0, 0))],
        out_specs=[spec, spec, spec],
        out_shape=[out, out, out],
        compiler_params=_params("parallel", "parallel"),
        name="peer_topk",
    )(q, sub_keys)


PEER_TM = 512
PEER_TE = 1024
PEER_SUB = 512
PEER_PASSES = 2
PEER_BUILD_UNROLL = 32
PEER_ROW_PAD = 8


def _peer_experts_kernel(h_ref, e1_ref, e2_ref, gt_ref, u_ref, v_ref, o_ref, gmat, e1s, e2s, gs):
    p = pl.program_id(1)
    c = pl.program_id(2)
    tm = h_ref.shape[0]
    te = v_ref.shape[0]
    slots = PEER_HEADS * PEER_TOPK
    pitch = gmat.shape[0] // tm
    kpp = pitch - PEER_ROW_PAD

    @pl.when((p == 0) & (c == 0))
    def _():
        o_ref[...] = jnp.zeros_like(o_ref)
        e1s[...] = e1_ref[...].reshape(slots, tm).T
        e2s[...] = e2_ref[...].reshape(slots, tm).T
        gs[...] = gt_ref[...].reshape(slots, tm).T

    @pl.when(c == 0)
    def _():
        key1 = (lax.broadcasted_iota(jnp.int32, (kpp, slots), 0) + p * kpp).astype(F32)
        key2 = lax.broadcasted_iota(jnp.int32, (PEER_KEYS, slots), 0).astype(F32)

        def body(t, carry):
            r1 = e1s[pl.ds(t, 1), :]
            r2 = e2s[pl.ds(t, 1), :]
            rg = gs[pl.ds(t, 1), :]
            a = jnp.where(key1 == r1, 1.0, 0.0).astype(BF16)
            b = jnp.where(key2 == r2, rg, 0.0).astype(BF16)
            gmat[pl.ds(pl.multiple_of(t * pitch, PEER_ROW_PAD), kpp), :] = _dot_nt(a, b)
            return carry

        lax.fori_loop(0, tm, body, 0, unroll=PEER_BUILD_UNROLL)

    per_step = te // PEER_KEYS
    per_sub = PEER_SUB // PEER_KEYS
    for sb in range(te // PEER_SUB):
        rows = slice(sb * PEER_SUB, (sb + 1) * PEER_SUB)
        a = _dot_nt(h_ref[...], u_ref[rows, :])
        act = 0.5 * a * (1.0 + lax.erf(a * (1.0 / math.sqrt(2.0))))
        g = jnp.concatenate([gmat[pl.ds(c * per_step + sb * per_sub + j, tm, stride=pitch), :]
                             for j in range(per_sub)], axis=1)
        o_ref[...] += _dot((act * g).astype(BF16), v_ref[rows, :])


def _peer_experts(h, e1, e2, gates, u, v):
    t, d = h.shape
    n_exp = v.shape[0]
    tm = min(PEER_TM, t)
    te = PEER_TE
    kpp = PEER_KEYS // PEER_PASSES
    cpp = n_exp // PEER_PASSES // te
    slots = PEER_HEADS * PEER_TOPK
    sel = pl.BlockSpec((PEER_HEADS, PEER_TOPK, tm), lambda i, p, c: (0, 0, i))
    return pl.pallas_call(
        _peer_experts_kernel,
        grid=(t // tm, PEER_PASSES, cpp),
        in_specs=[pl.BlockSpec((tm, d), lambda i, p, c: (i, 0)),
                  sel, sel, sel,
                  pl.BlockSpec((te, d), lambda i, p, c: (p * cpp + c, 0)),
                  pl.BlockSpec((te, d), lambda i, p, c: (p * cpp + c, 0))],
        out_specs=pl.BlockSpec((tm, d), lambda i, p, c: (i, 0)),
        out_shape=jax.ShapeDtypeStruct((t, d), F32),
        scratch_shapes=[pltpu.VMEM((tm * (kpp + PEER_ROW_PAD), PEER_KEYS), F32)]
        + [pltpu.VMEM((tm, slots), F32)] * 3,
        compiler_params=_params("parallel", "arbitrary", "arbitrary"),
        name="peer_experts",
    )(h, e1, e2, gates, u, v)


def _peer(h, w_q, sub_keys, u, v):
    q = _mm(h, w_q.astype(BF16), BF16)
    e1, e2, gates = _peer_topk(q, sub_keys.astype(BF16))
    return _peer_experts(h, e1, e2, gates, u.astype(BF16), v.astype(BF16))


def kernel(x, c, ada_w, ada_b, ln_g, ln_b, fp_w_in, fp_w_fourier, fp_w_pool, fp_pool_scale, fp_w_out,
           at_w_in, at_q_norm, at_k_norm, at_rpb, at_w_out, peer_w_q, peer_sub_keys, peer_u, peer_v):
    batch, seq, d = x.shape
    depth = ada_w.shape[0]
    alpha = (2.0 * depth) ** 0.25
    xt = x.reshape(batch * seq, d)
    ada = _adaln_all(c, ada_w, ada_b)[:, :batch]
    shift = ada[:, :, None, :d]
    scale = ada[:, :, None, d:2 * d]
    gate = ada[:, :, None, 2 * d:]
    h = _modulate(xt, shift[0], scale[0], seq)
    for n in range(2 * depth):
        l, sub = divmod(n, 2)
        i = l // 2
        if sub == 1:
            y = _peer(h, peer_w_q[l], peer_sub_keys[l], peer_u[l], peer_v[l])
        elif l % 2 == 0:
            y = _fourier_pool_mixer(h, fp_w_in[i], fp_w_fourier[i], fp_w_pool[i], fp_pool_scale[i],
                                    fp_w_out[i], batch, seq)
        else:
            y = _attention_mixer(h, at_w_in[i], at_q_norm[i], at_k_norm[i], at_rpb[i], at_w_out[i],
                                 batch, seq)
        if n + 1 < 2 * depth:
            xt, h = _postnorm_modulate(xt, y, gate[n], ln_g[l, sub], ln_b[l, sub],
                                       shift[n + 1], scale[n + 1], seq, alpha)
        else:
            xt = _postnorm(xt, y, gate[n], ln_g[l, sub], ln_b[l, sub], seq, alpha)
    return xt.reshape(batch, seq, d)
```

```python
import functools
import math

import numpy as np
import jax
import jax.numpy as jnp
from jax import lax
from jax.experimental import pallas as pl
from jax.experimental.pallas import tpu as pltpu

F32 = jnp.float32
BF16 = jnp.bfloat16

GRID_COLS = 64
HEAD = 128
GROUP_W = 256
N_FOURIER = 4
N_POOL = 4
POOL_WINDOWS = (2, 4, 8, 16)
N_Q_HEADS = 8
N_KV_HEADS = 2
N_NA_HEADS = 8
NA_ROWS = 8
NA_COLS = 16
ROPE_THETA = 10000.0
PEER_HEADS = 8
PEER_KEYS = 128
PEER_TOPK = 16
LN_EPS = 1e-6
FFT_N1 = 128
FFT_K1_BLOCK = 8

VMEM_LIMIT_BYTES = 56 * 1024 * 1024
LANES = 128
NEG_BIG = -1e30


def _params(*sem):
    return pltpu.CompilerParams(dimension_semantics=sem, vmem_limit_bytes=VMEM_LIMIT_BYTES)


def _dot(a, b):
    return jnp.dot(a, b, preferred_element_type=F32)


def _dot_nt(a, b):
    return lax.dot_general(a, b, (((1,), (1,)), ((), ())), preferred_element_type=F32)


def _layer_norm(x):
    mu = jnp.mean(x, axis=-1, keepdims=True)
    xc = x - mu
    var = jnp.mean(xc * xc, axis=-1, keepdims=True)
    return xc * lax.rsqrt(var + LN_EPS)


def _adaln_kernel(c_ref, w_ref, b_ref, o_ref):
    c = c_ref[...].astype(BF16)
    w = w_ref[0].astype(BF16)
    o_ref[0] = _dot(c, w) + b_ref[0]


def _adaln_all(c, ada_w, ada_b):
    depth, two, d, d3 = ada_w.shape
    n = depth * two
    b = c.shape[0]
    c_pad = jnp.zeros((8, d), F32).at[:b].set(c)
    w = ada_w.reshape(n, d, d3)
    bias = ada_b.reshape(n, 1, d3)
    tn = 768
    return pl.pallas_call(
        _adaln_kernel,
        grid=(n, d3 // tn),
        in_specs=[pl.BlockSpec((8, d), lambda i, j: (0, 0)),
                  pl.BlockSpec((1, d, tn), lambda i, j: (i, 0, j)),
                  pl.BlockSpec((1, 1, tn), lambda i, j: (i, 0, j))],
        out_specs=pl.BlockSpec((1, 8, tn), lambda i, j: (i, 0, j)),
        out_shape=jax.ShapeDtypeStruct((n, 8, d3), F32),
        compiler_params=_params("parallel", "parallel"),
        name="adaln",
    )(c_pad, w, bias)


def _modulate_kernel(x_ref, sh_ref, sc_ref, o_ref):
    y = _layer_norm(x_ref[...])
    o_ref[...] = (y * (1.0 + sc_ref[0]) + sh_ref[0]).astype(o_ref.dtype)


def _modulate(x, shift, scale, seq):
    t, d = x.shape
    tm = 512
    per_b = seq // tm
    return pl.pallas_call(
        _modulate_kernel,
        grid=(t // tm,),
        in_specs=[pl.BlockSpec((tm, d), lambda i: (i, 0)),
                  pl.BlockSpec((1, 1, d), lambda i: (i // per_b, 0, 0)),
                  pl.BlockSpec((1, 1, d), lambda i: (i // per_b, 0, 0))],
        out_specs=pl.BlockSpec((tm, d), lambda i: (i, 0)),
        out_shape=jax.ShapeDtypeStruct((t, d), BF16),
        compiler_params=_params("parallel"),
        name="modulate",
    )(x, shift, scale)


def _postnorm_kernel(x_ref, y_ref, gate_ref, g_ref, b_ref, o_ref, *, alpha):
    u = alpha * x_ref[...] + (1.0 + gate_ref[0]) * y_ref[...]
    o_ref[...] = _layer_norm(u) * g_ref[...] + b_ref[...]


def _postnorm(x, y, gate, g, b, seq, alpha):
    t, d = x.shape
    tm = 512
    per_b = seq // tm
    return pl.pallas_call(
        functools.partial(_postnorm_kernel, alpha=alpha),
        grid=(t // tm,),
        in_specs=[pl.BlockSpec((tm, d), lambda i: (i, 0)),
                  pl.BlockSpec((tm, d), lambda i: (i, 0)),
                  pl.BlockSpec((1, 1, d), lambda i: (i // per_b, 0, 0)),
                  pl.BlockSpec((1, d), lambda i: (0, 0)),
                  pl.BlockSpec((1, d), lambda i: (0, 0))],
        out_specs=pl.BlockSpec((tm, d), lambda i: (i, 0)),
        out_shape=jax.ShapeDtypeStruct((t, d), F32),
        compiler_params=_params("parallel"),
        name="postnorm",
    )(x, y, gate, g.reshape(1, d), b.reshape(1, d))


def _postnorm_modulate_kernel(x_ref, y_ref, gate_ref, g_ref, b_ref, sh_ref, sc_ref, o_ref, h_ref, *, alpha):
    u = alpha * x_ref[...] + (1.0 + gate_ref[0]) * y_ref[...]
    xn = _layer_norm(u) * g_ref[...] + b_ref[...]
    o_ref[...] = xn
    h_ref[...] = (_layer_norm(xn) * (1.0 + sc_ref[0]) + sh_ref[0]).astype(h_ref.dtype)


def _postnorm_modulate(x, y, gate, g, b, shift, scale, seq, alpha):
    t, d = x.shape
    tm = 512
    per_b = seq // tm
    row = pl.BlockSpec((tm, d), lambda i: (i, 0))
    per_batch = pl.BlockSpec((1, 1, d), lambda i: (i // per_b, 0, 0))
    vec = pl.BlockSpec((1, d), lambda i: (0, 0))
    return pl.pallas_call(
        functools.partial(_postnorm_modulate_kernel, alpha=alpha),
        grid=(t // tm,),
        in_specs=[row, row, per_batch, vec, vec, per_batch, per_batch],
        out_specs=[row, row],
        out_shape=[jax.ShapeDtypeStruct((t, d), F32), jax.ShapeDtypeStruct((t, d), BF16)],
        compiler_params=_params("parallel"),
        name="postnorm_modulate",
    )(x, y, gate, g.reshape(1, d), b.reshape(1, d), shift, scale)


def _mm_kernel(a_ref, b_ref, o_ref):
    o_ref[...] = _dot(a_ref[...], b_ref[...]).astype(o_ref.dtype)


def _mm(a, w, out_dtype, tm=512, tn=1024):
    m, k = a.shape
    n = w.shape[1]
    tm = min(tm, m)
    tn = min(tn, n)
    return pl.pallas_call(
        _mm_kernel,
        grid=(n // tn, m // tm),
        in_specs=[pl.BlockSpec((tm, k), lambda j, i: (i, 0)),
                  pl.BlockSpec((k, tn), lambda j, i: (0, j))],
        out_specs=pl.BlockSpec((tm, tn), lambda j, i: (i, j)),
        out_shape=jax.ShapeDtypeStruct((m, n), out_dtype),
        compiler_params=_params("parallel", "parallel"),
        name="mm",
    )(a, w)


def _mm2_kernel(a1_ref, a2_ref, w1_ref, w2_ref, o_ref):
    acc = _dot(a1_ref[...].astype(BF16), w1_ref[...])
    acc = acc + _dot(a2_ref[...].astype(BF16), w2_ref[...])
    o_ref[...] = acc


def _mm2(a1, a2, w, tm=512, tn=1024):
    m, k = a1.shape
    n = w.shape[1]
    tm = min(tm, m)
    tn = min(tn, n)
    return pl.pallas_call(
        _mm2_kernel,
        grid=(n // tn, m // tm),
        in_specs=[pl.BlockSpec((tm, k), lambda j, i: (i, 0)),
                  pl.BlockSpec((tm, k), lambda j, i: (i, 0)),
                  pl.BlockSpec((k, tn), lambda j, i: (0, j)),
                  pl.BlockSpec((k, tn), lambda j, i: (1, j))],
        out_specs=pl.BlockSpec((tm, tn), lambda j, i: (i, j)),
        out_shape=jax.ShapeDtypeStruct((m, n), F32),
        compiler_params=_params("parallel", "parallel"),
        name="mm2",
    )(a1, a2, w, w)


def _fourier_proj_kernel(z_ref, w_ref, a_ref, b_ref):
    r = _dot(z_ref[...], w_ref[0])
    a_ref[...] = r[:, :GROUP_W].astype(a_ref.dtype)
    b_ref[...] = r[:, GROUP_W:].astype(b_ref.dtype)


def _fourier_proj(z, w_ab):
    t = z.shape[0]
    tm = 1024
    wid = N_FOURIER * GROUP_W
    return pl.pallas_call(
        _fourier_proj_kernel,
        grid=(N_FOURIER, t // tm),
        in_specs=[pl.BlockSpec((tm, GROUP_W), lambda g, i: (i, g)),
                  pl.BlockSpec((1, GROUP_W, 2 * GROUP_W), lambda g, i: (g, 0, 0))],
        out_specs=[pl.BlockSpec((tm, GROUP_W), lambda g, i: (i, g)),
                   pl.BlockSpec((tm, GROUP_W), lambda g, i: (i, g))],
        out_shape=[jax.ShapeDtypeStruct((t, wid), BF16), jax.ShapeDtypeStruct((t, wid), BF16)],
        compiler_params=_params("parallel", "parallel"),
        name="fourier_proj",
    )(z, w_ab)


def _fft1_kernel(a_ref, b_ref, c_ref, s_ref, twc_ref, tws_ref, tr_ref, ti_ref):
    a = a_ref[0]
    b = b_ref[0]
    c = c_ref[...]
    s = s_ref[...]
    tr = _dot(c, a) + _dot(s, b)
    ti = _dot(c, b) - _dot(s, a)
    reps = a.shape[1] // LANES
    wc = jnp.concatenate([twc_ref[0]] * reps, axis=1)
    ws = jnp.concatenate([tws_ref[0]] * reps, axis=1)
    tr_ref[0] = (tr * wc + ti * ws).astype(tr_ref.dtype)
    ti_ref[0] = (ti * wc - tr * ws).astype(ti_ref.dtype)


def _fft2_kernel(tr_ref, ti_ref, lr_ref, li_ref, o_ref):
    kb, n2, cols = tr_ref.shape[1:]
    tr = tr_ref[0].reshape(kb * n2, cols)
    ti = ti_ref[0].reshape(kb * n2, cols)
    res = _dot(lr_ref[...], tr) + _dot(li_ref[...], ti)
    o_ref[0] = res.reshape(n2, kb, cols)


def _seq_dft_real(a, b, batch, seq):
    wid = a.shape[1]
    n1 = FFT_N1
    n2 = seq // n1
    kb = FFT_K1_BLOCK
    k = np.arange(n1)
    ang1 = 2.0 * np.pi * np.outer(k, k) / n1
    c1 = jnp.asarray(np.cos(ang1), BF16)
    s1 = jnp.asarray(np.sin(ang1), BF16)
    angt = 2.0 * np.pi * np.outer(np.arange(n2), k) / seq
    twc = jnp.asarray(np.broadcast_to(np.cos(angt)[:, :, None], (n2, n1, LANES)), F32)
    tws = jnp.asarray(np.broadcast_to(np.sin(angt)[:, :, None], (n2, n1, LANES)), F32)
    a3 = a.reshape(batch, n1, n2 * wid)
    b3 = b.reshape(batch, n1, n2 * wid)
    tr, ti = pl.pallas_call(
        _fft1_kernel,
        grid=(batch, n2),
        in_specs=[pl.BlockSpec((1, n1, wid), lambda bb, j: (bb, 0, j)),
                  pl.BlockSpec((1, n1, wid), lambda bb, j: (bb, 0, j)),
                  pl.BlockSpec((n1, n1), lambda bb, j: (0, 0)),
                  pl.BlockSpec((n1, n1), lambda bb, j: (0, 0)),
                  pl.BlockSpec((1, n1, LANES), lambda bb, j: (j, 0, 0)),
                  pl.BlockSpec((1, n1, LANES), lambda bb, j: (j, 0, 0))],
        out_specs=[pl.BlockSpec((1, n1, wid), lambda bb, j: (bb, 0, j)),
                   pl.BlockSpec((1, n1, wid), lambda bb, j: (bb, 0, j))],
        out_shape=[jax.ShapeDtypeStruct((batch, n1, n2 * wid), BF16)] * 2,
        compiler_params=_params("parallel", "parallel"),
        name="fft_stage1",
    )(a3, b3, c1, s1, twc, tws)
    k2 = np.arange(n2)
    ang2 = 2.0 * np.pi * np.outer(k2, k2) / n2
    eye = np.eye(kb)
    lr = np.einsum('kn,jJ->kjJn', np.cos(ang2), eye).reshape(n2 * kb, kb * n2)
    li = np.einsum('kn,jJ->kjJn', np.sin(ang2), eye).reshape(n2 * kb, kb * n2)
    tr4 = tr.reshape(batch, n1, n2, wid)
    ti4 = ti.reshape(batch, n1, n2, wid)
    out = pl.pallas_call(
        _fft2_kernel,
        grid=(batch, n1 // kb),
        in_specs=[pl.BlockSpec((1, kb, n2, wid), lambda bb, j: (bb, j, 0, 0)),
                  pl.BlockSpec((1, kb, n2, wid), lambda bb, j: (bb, j, 0, 0)),
                  pl.BlockSpec((n2 * kb, kb * n2), lambda bb, j: (0, 0)),
                  pl.BlockSpec((n2 * kb, kb * n2), lambda bb, j: (0, 0))],
        out_specs=pl.BlockSpec((1, n2, kb, wid), lambda bb, j: (bb, 0, j, 0)),
        out_shape=jax.ShapeDtypeStruct((batch, n2, n1, wid), F32),
        compiler_params=_params("parallel", "parallel"),
        name="fft_stage2",
    )(tr4, ti4, jnp.asarray(lr, BF16), jnp.asarray(li, BF16))
    return out.reshape(batch * seq, wid)


def _fourier_weights(w_fourier, seq):
    g, cg, _ = w_fourier.shape
    k = np.arange(cg)
    ang = 2.0 * np.pi * np.outer(k, k) / cg
    norm = 1.0 / math.sqrt(seq * cg)
    cs = jnp.asarray(np.concatenate([np.cos(ang), -np.sin(ang)], axis=0) * norm, BF16)
    wf = jnp.transpose(w_fourier, (1, 0, 2)).reshape(cg, g * cg).astype(BF16)
    r = _mm(cs, wf, F32)
    r = r.reshape(2, cg, g, cg)
    return jnp.concatenate([r[0], r[1]], axis=-1).transpose(1, 0, 2).astype(BF16)


POOL_TM = 256
POOL_PAD = 128


def _pool_kernel(z_ref, band_ref, w_ref, ps_ref, o_ref, zp_ref, *, win, seq):
    cg = z_ref.shape[2]
    zp_ref[0:POOL_PAD, :] = jnp.zeros((POOL_PAD, cg), BF16)
    zp_ref[seq + POOL_PAD:seq + 2 * POOL_PAD, :] = jnp.zeros((POOL_PAD, cg), BF16)
    zp_ref[POOL_PAD:seq + POOL_PAD, :] = z_ref[0]
    band = band_ref[...]
    w = w_ref[0]
    ps = ps_ref[0]
    half = win // 2

    def body(i, carry):
        t0 = pl.multiple_of(i * POOL_TM, POOL_TM)
        slab = zp_ref[pl.ds(t0, POOL_TM + 2 * POOL_PAD), :]
        sums = _dot(band, slab)
        t = t0 + lax.broadcasted_iota(jnp.int32, (POOL_TM, cg), 0)
        cnt = (jnp.minimum(t + half, seq) - jnp.maximum(t - half, 0)).astype(F32)
        zc = zp_ref[pl.ds(t0 + POOL_PAD, POOL_TM), :].astype(F32)
        p = sums / cnt - zc
        y = _dot(p.astype(BF16), w) * ps
        o_ref[0, pl.ds(t0, POOL_TM), :] = y.astype(o_ref.dtype)
        return carry

    lax.fori_loop(0, seq // POOL_TM, body, 0)


def _pool_group(z3, w_pool, pool_scale, g, batch, seq):
    win = POOL_WINDOWS[g]
    half = win // 2
    tau = np.arange(POOL_TM)[:, None]
    kap = np.arange(POOL_TM + 2 * POOL_PAD)[None, :]
    off = kap - POOL_PAD - tau
    band = jnp.asarray(((off >= -half) & (off < half)).astype(np.float32), BF16)
    col = N_FOURIER + g
    return pl.pallas_call(
        functools.partial(_pool_kernel, win=win, seq=seq),
        grid=(batch,),
        in_specs=[pl.BlockSpec((1, seq, GROUP_W), lambda bb: (bb, 0, col)),
                  pl.BlockSpec((POOL_TM, POOL_TM + 2 * POOL_PAD), lambda bb: (0, 0)),
                  pl.BlockSpec((1, GROUP_W, GROUP_W), lambda bb: (g, 0, 0)),
                  pl.BlockSpec((1, 1, GROUP_W), lambda bb: (g, 0, 0))],
        out_specs=pl.BlockSpec((1, seq, GROUP_W), lambda bb: (bb, 0, 0)),
        out_shape=jax.ShapeDtypeStruct((batch, seq, GROUP_W), BF16),
        scratch_shapes=[pltpu.VMEM((seq + 2 * POOL_PAD, GROUP_W), BF16)],
        compiler_params=_params("parallel"),
        name=f"pool_w{win}",
    )(z3, band, w_pool, pool_scale)


def _fourier_pool_mixer(h, w_in, w_fourier, w_pool, pool_scale, w_out, batch, seq):
    t = h.shape[0]
    z = _mm(h, w_in.astype(BF16), BF16)
    a, b = _fourier_proj(z, _fourier_weights(w_fourier, seq))
    yf = _seq_dft_real(a, b, batch, seq)
    z3 = z.reshape(batch, seq, z.shape[1])
    wp = w_pool.astype(BF16)
    ps = pool_scale.reshape(N_POOL, 1, GROUP_W)
    yp = jnp.concatenate([_pool_group(z3, wp, ps, g, batch, seq) for g in range(N_POOL)], axis=-1)
    return _mm2(yf, yp.reshape(t, N_POOL * GROUP_W), w_out.astype(BF16))


def _qk_prep_kernel(x_ref, g_ref, cos_ref, sin_ref, o_ref):
    x = x_ref[...].astype(F32)
    y = x * lax.rsqrt(jnp.mean(x * x, axis=-1, keepdims=True) + LN_EPS) * g_ref[0]
    lane = lax.broadcasted_iota(jnp.int32, y.shape, 1)
    first = (lane % (HEAD // 2)) < (HEAD // 4)
    partner = jnp.where(first, pltpu.roll(y, HEAD - HEAD // 4, 1), pltpu.roll(y, HEAD // 4, 1))
    o_ref[...] = (y * cos_ref[...] + partner * sin_ref[...]).astype(o_ref.dtype)


def _rope_tables(seq):
    quarter = HEAD // 4
    inv = ROPE_THETA ** (-np.arange(quarter, dtype=np.float64) / quarter)
    t = np.arange(seq)
    ang_r = (t // GRID_COLS)[:, None] * inv[None, :]
    ang_c = (t % GRID_COLS)[:, None] * inv[None, :]
    cos = np.concatenate([np.cos(ang_r), np.cos(ang_r), np.cos(ang_c), np.cos(ang_c)], axis=1)
    sin = np.concatenate([-np.sin(ang_r), np.sin(ang_r), -np.sin(ang_c), np.sin(ang_c)], axis=1)
    return jnp.asarray(cos, F32), jnp.asarray(sin, F32)


def _qk_prep(proj, gains, seq):
    t = proj.shape[0]
    nh = gains.shape[0]
    tm = 1024
    per_b = seq // tm
    cos, sin = _rope_tables(seq)
    return pl.pallas_call(
        _qk_prep_kernel,
        grid=(t // tm, nh),
        in_specs=[pl.BlockSpec((tm, HEAD), lambda i, hh: (i, hh)),
                  pl.BlockSpec((1, 1, HEAD), lambda i, hh: (hh, 0, 0)),
                  pl.BlockSpec((tm, HEAD), lambda i, hh: (i % per_b, 0)),
                  pl.BlockSpec((tm, HEAD), lambda i, hh: (i % per_b, 0))],
        out_specs=pl.BlockSpec((tm, HEAD), lambda i, hh: (i, hh)),
        out_shape=jax.ShapeDtypeStruct((t, nh * HEAD), BF16),
        compiler_params=_params("parallel", "parallel"),
        name="qk_prep",
    )(proj, gains, cos, sin)


FLASH_TQ = 512
FLASH_TK = 1024
FLASH_UNROLL = 4


def _flash_kernel(q_ref, k_ref, v_ref, o_ref):
    seq = k_ref.shape[0]
    tq = q_ref.shape[0]
    for hh in range(N_Q_HEADS // N_KV_HEADS):
        q = q_ref[:, hh * HEAD:(hh + 1) * HEAD]

        def body(c, carry):
            m, l, acc = carry
            c0 = pl.multiple_of(c * FLASH_TK, FLASH_TK)
            k = k_ref[pl.ds(c0, FLASH_TK), :]
            v = v_ref[pl.ds(c0, FLASH_TK), :]
            s = _dot_nt(q, k)
            m_new = jnp.maximum(m, jnp.max(s, axis=-1, keepdims=True))
            p = jnp.exp2(s - m_new)
            corr = jnp.exp2(m - m_new)
            l = l * corr + jnp.sum(p, axis=-1, keepdims=True)
            acc = acc * corr + _dot(p.astype(BF16), v)
            return m_new, l, acc

        init = (jnp.full((tq, 1), NEG_BIG, F32), jnp.zeros((tq, 1), F32), jnp.zeros((tq, HEAD), F32))
        m, l, acc = lax.fori_loop(0, seq // FLASH_TK, body, init, unroll=FLASH_UNROLL)
        o_ref[:, hh * HEAD:(hh + 1) * HEAD] = (acc / l).astype(o_ref.dtype)


def _gqa_attention(qk, proj, batch, seq):
    t = qk.shape[0]
    per_b = seq // FLASH_TQ
    gw = (N_Q_HEADS // N_KV_HEADS) * HEAD
    k_col = N_Q_HEADS
    v_col = N_Q_HEADS + N_KV_HEADS
    return pl.pallas_call(
        _flash_kernel,
        grid=(batch, N_KV_HEADS, per_b),
        in_specs=[pl.BlockSpec((FLASH_TQ, gw), lambda bb, kv, i: (bb * per_b + i, kv)),
                  pl.BlockSpec((seq, HEAD), lambda bb, kv, i: (bb, k_col + kv)),
                  pl.BlockSpec((seq, HEAD), lambda bb, kv, i: (bb, v_col + kv))],
        out_specs=pl.BlockSpec((FLASH_TQ, gw), lambda bb, kv, i: (bb * per_b + i, kv)),
        out_shape=jax.ShapeDtypeStruct((t, N_Q_HEADS * HEAD), BF16),
        compiler_params=_params("parallel", "parallel", "parallel"),
        name="gqa_flash",
    )(qk, qk, proj)


NA_ROW_BLOCK = 8


def _na_kernel(q_ref, k_ref, v_ref, bias_ref, o_ref, *, rows, scale):
    i = pl.program_id(2)
    span = NA_ROWS * GRID_COLS
    starts, scores = [], []
    for rr in range(NA_ROW_BLOCK):
        r = i * NA_ROW_BLOCK + rr
        rs = jnp.clip(r - NA_ROWS // 2, 0, rows - NA_ROWS)
        dr0 = rs - r + (NA_ROWS - 1)
        k0 = pl.multiple_of(rs * GRID_COLS, GRID_COLS)
        q = q_ref[rr * GRID_COLS:(rr + 1) * GRID_COLS, :]
        starts.append(k0)
        scores.append(_dot_nt(q, k_ref[pl.ds(k0, span), :]) * scale + bias_ref[0, dr0])
    probs = []
    for s in scores:
        p = jnp.exp(s - jnp.max(s, axis=-1, keepdims=True))
        probs.append((p / jnp.sum(p, axis=-1, keepdims=True)).astype(BF16))
    outs = [_dot(p, v_ref[pl.ds(k0, span), :]) for p, k0 in zip(probs, starts)]
    o_ref[...] = jnp.concatenate(outs, axis=0).astype(o_ref.dtype)


def _na_bias_table(rpb):
    cols = np.arange(GRID_COLS)
    start = np.clip(cols - NA_COLS // 2, 0, GRID_COLS - NA_COLS)
    kc = np.arange(GRID_COLS)
    valid = (kc[None, :] >= start[:, None]) & (kc[None, :] < start[:, None] + NA_COLS)
    off = np.clip(kc[None, :] - cols[:, None] + (NA_COLS - 1), 0, 2 * NA_COLS - 2)
    tab = rpb[:, :, off]
    tab = jnp.where(jnp.asarray(valid)[None, None], tab, NEG_BIG)
    per_dr0 = [jnp.concatenate([tab[:, d + j] for j in range(NA_ROWS)], axis=-1) for d in range(NA_ROWS)]
    return jnp.stack(per_dr0, axis=1).astype(F32)


def _neighbourhood_attention(proj, rpb, batch, seq):
    t = proj.shape[0]
    rows = seq // GRID_COLS
    qb = NA_ROW_BLOCK * GRID_COLS
    per_b = seq // qb
    q_col = (N_Q_HEADS + 2 * N_KV_HEADS)
    k_col = q_col + N_NA_HEADS
    v_col = k_col + N_NA_HEADS
    bias = _na_bias_table(rpb)
    return pl.pallas_call(
        functools.partial(_na_kernel, rows=rows, scale=HEAD ** -0.5),
        grid=(batch, N_NA_HEADS, per_b),
        in_specs=[pl.BlockSpec((qb, HEAD), lambda bb, hh, i: (bb * per_b + i, q_col + hh)),
                  pl.BlockSpec((seq, HEAD), lambda bb, hh, i: (bb, k_col + hh)),
                  pl.BlockSpec((seq, HEAD), lambda bb, hh, i: (bb, v_col + hh)),
                  pl.BlockSpec((1, NA_ROWS, GRID_COLS, NA_ROWS * GRID_COLS), lambda bb, hh, i: (hh, 0, 0, 0))],
        out_specs=pl.BlockSpec((qb, HEAD), lambda bb, hh, i: (bb * per_b + i, hh)),
        out_shape=jax.ShapeDtypeStruct((t, N_NA_HEADS * HEAD), BF16),
        compiler_params=_params("parallel", "parallel", "parallel"),
        name="natten",
    )(proj, proj, proj, bias)


def _attention_mixer(h, w_in, q_norm, k_norm, rpb, w_out, batch, seq):
    proj = _mm(h, w_in.astype(BF16), BF16, tn=1152)
    q_gain = HEAD ** -0.5 * math.log2(math.e)
    gains = jnp.concatenate([jnp.tile(q_norm[None] * q_gain, (N_Q_HEADS, 1)),
                             jnp.tile(k_norm[None], (N_KV_HEADS, 1))], axis=0)
    qk = _qk_prep(proj, gains.reshape(N_Q_HEADS + N_KV_HEADS, 1, HEAD).astype(F32), seq)
    yc = _gqa_attention(qk, proj, batch, seq)
    yd = _neighbourhood_attention(proj, rpb, batch, seq)
    return _mm2(yc, yd, w_out.astype(BF16))


def _row(v, i, fill):
    rows = lax.broadcasted_iota(jnp.int32, v.shape, 0)
    return jnp.max(jnp.where(rows == i, v, fill), axis=0, keepdims=True)


def _top16(s, ids, big):
    rows, n = s.shape
    half = rows // 2
    id_a, id_b = ids
    by_id = jnp.where(id_a < id_b, 1.0, 0.0)
    first = jnp.where(s[:half] > s[half:], 1.0, jnp.where(s[:half] < s[half:], 0.0, by_id)) > 0.5
    top = jnp.where(first, s[:half], s[half:])
    low = jnp.where(first, s[half:], s[:half])
    top_id = jnp.where(first, id_a, id_b)
    low_id = jnp.where(first, id_b, id_a)
    slot = lax.broadcasted_iota(jnp.int32, (PEER_TOPK, n), 0)
    vals = jnp.zeros((PEER_TOPK, n), F32)
    idxs = jnp.zeros((PEER_TOPK, n), F32)
    for it in range(PEER_TOPK):
        m = jnp.max(top, axis=0, keepdims=True)
        ix = jnp.min(jnp.where(top == m, top_id, big), axis=0, keepdims=True)
        hit = top_id == ix
        top = jnp.where(hit, low, top)
        top_id = jnp.where(hit, low_id, top_id)
        low = jnp.where(hit, -jnp.inf, low)
        vals = jnp.where(slot == it, m, vals)
        idxs = jnp.where(slot == it, ix, idxs)
    return vals, idxs


def _pick(table, sel):
    out = jnp.zeros_like(table)
    for r in range(PEER_TOPK):
        out = jnp.where(sel == r, _row(table, r, -1.0), out)
    return out


def _peer_topk_kernel(q_ref, keys_ref, e1_ref, e2_ref, g_ref):
    tm = q_ref.shape[0]
    half = q_ref.shape[1] // 2
    k = PEER_TOPK
    s1 = _dot_nt(keys_ref[0, 0], q_ref[:, :half])
    s2 = _dot_nt(keys_ref[0, 1], q_ref[:, half:])
    key_lo = lax.broadcasted_iota(jnp.int32, (PEER_KEYS // 2, tm), 0).astype(F32)
    key_ids = (key_lo, key_lo + float(PEER_KEYS // 2))
    v1, i1 = _top16(s1, key_ids, float(PEER_KEYS))
    v2, i2 = _top16(s2, key_ids, float(PEER_KEYS))
    i8 = lax.broadcasted_iota(jnp.int32, (8, tm), 0)
    i16 = lax.broadcasted_iota(jnp.int32, (k, tm), 0)
    ninf = -jnp.inf
    cand = [_row(v1, 0, ninf) + v2]
    ids = [i16]
    for i in (1, 2, 3):
        cand.append(_row(v1, i, ninf) + v2[0:8])
        ids.append(i * k + i8)
    cand.append(v1[8:16] + _row(v2, 0, ninf))
    ids.append((i8 + 8) * k)
    for j in (0, 1):
        cand.append(jnp.where(i8 < 4, ninf, v1[0:8] + _row(v2, j, ninf)))
        ids.append(i8 * k + j)
    cand = jnp.concatenate(cand, axis=0)
    ids = (jnp.concatenate(ids[:3], axis=0).astype(F32), jnp.concatenate(ids[3:], axis=0).astype(F32))
    sc, cf = _top16(cand, ids, float(k * k))
    ci = cf.astype(jnp.int32)
    ex = jnp.exp(sc - _row(sc, 0, ninf))
    e1_ref[0] = _pick(i1, lax.shift_right_logical(ci, 4))
    e2_ref[0] = _pick(i2, lax.bitwise_and(ci, k - 1))
    g_ref[0] = ex / jnp.sum(ex, axis=0, keepdims=True)


def _peer_topk(q, sub_keys):
    t = q.shape[0]
    tm = 256
    qd = q.shape[1] // PEER_HEADS
    out = jax.ShapeDtypeStruct((PEER_HEADS, PEER_TOPK, t), F32)
    spec = pl.BlockSpec((1, PEER_TOPK, tm), lambda i, hh: (hh, 0, i))
    return pl.pallas_call(
        _peer_topk_kernel,
        grid=(t // tm, PEER_HEADS),
        in_specs=[pl.BlockSpec((tm, qd), lambda i, hh: (i, hh)),
                  pl.BlockSpec((1, 2, PEER_KEYS, qd // 2), lambda i, hh: (hh, 0, 0, 0))],
        out_specs=[spec, spec, spec],
        out_shape=[out, out, out],
        compiler_params=_params("parallel", "parallel"),
        name="peer_topk",
    )(q, sub_keys)


PEER_TM = 512
PEER_SUB = 512
PEER_BUILD_UNROLL = 32
PEER_ROW_PAD = 8
HI16 = 0xFFFF0000


def _peer_experts_kernel(h_ref, e1_ref, e2_ref, gt_ref, ulo_ref, uhi_ref, vlo_ref, vhi_ref, o_ref,
                         gmat, e1s, e2s, gs):
    c = pl.program_id(1)
    tm = h_ref.shape[0]
    slots = PEER_HEADS * PEER_TOPK
    pitch = gmat.shape[0] // tm
    half = pitch - PEER_ROW_PAD

    @pl.when(c == 0)
    def _():
        o_ref[...] = jnp.zeros_like(o_ref)
        e1s[...] = e1_ref[...].reshape(slots, tm).T
        e2s[...] = e2_ref[...].reshape(slots, tm).T
        gs[...] = gt_ref[...].reshape(slots, tm).T
        key = lax.broadcasted_iota(jnp.int32, (PEER_KEYS, slots), 0).astype(F32)

        def body(t, carry):
            r1 = e1s[pl.ds(t, 1), :]
            r2 = e2s[pl.ds(t, 1), :]
            rg = gs[pl.ds(t, 1), :]
            a = jnp.where(key == r1, 1.0, 0.0).astype(BF16)
            b = jnp.where(key == r2, rg, 0.0).astype(BF16)
            g = _dot_nt(a, b).astype(BF16).astype(F32)
            hi = pltpu.bitcast(g[:half], jnp.uint32)
            lo = pltpu.bitcast(g[half:], jnp.uint32)
            gmat[pl.ds(pl.multiple_of(t * pitch, PEER_ROW_PAD), half), :] = hi | (lo >> 16)
            return carry

        lax.fori_loop(0, tm, body, 0, unroll=PEER_BUILD_UNROLL)

    per_sub = ulo_ref.shape[0] // PEER_KEYS
    words = [gmat[pl.ds(c * per_sub + j, tm, stride=pitch), :] for j in range(per_sub)]
    g_lo = jnp.concatenate([pltpu.bitcast(w & jnp.uint32(HI16), F32) for w in words], axis=1)
    g_hi = jnp.concatenate([pltpu.bitcast(w << 16, F32) for w in words], axis=1)
    for u_ref, v_ref, g in ((ulo_ref, vlo_ref, g_lo), (uhi_ref, vhi_ref, g_hi)):
        a = _dot_nt(h_ref[...], u_ref[...])
        act = 0.5 * a * (1.0 + lax.erf(a * (1.0 / math.sqrt(2.0))))
        o_ref[...] += _dot((act * g).astype(BF16), v_ref[...])


def _peer_experts(h, e1, e2, gates, u, v):
    t, d = h.shape
    n_exp = v.shape[0]
    tm = min(PEER_TM, t)
    half = PEER_KEYS // 2
    steps = n_exp // 2 // PEER_SUB
    slots = PEER_HEADS * PEER_TOPK
    sel = pl.BlockSpec((PEER_HEADS, PEER_TOPK, tm), lambda i, c: (0, 0, i))
    lo_blk = pl.BlockSpec((PEER_SUB, d), lambda i, c: (c, 0))
    hi_blk = pl.BlockSpec((PEER_SUB, d), lambda i, c: (steps + c, 0))
    return pl.pallas_call(
        _peer_experts_kernel,
        grid=(t // tm, steps),
        in_specs=[pl.BlockSpec((tm, d), lambda i, c: (i, 0)), sel, sel, sel, lo_blk, hi_blk, lo_blk, hi_blk],
        out_specs=pl.BlockSpec((tm, d), lambda i, c: (i, 0)),
        out_shape=jax.ShapeDtypeStruct((t, d), F32),
        scratch_shapes=[pltpu.VMEM((tm * (half + PEER_ROW_PAD), PEER_KEYS), jnp.uint32)]
        + [pltpu.VMEM((tm, slots), F32)] * 3,
        compiler_params=_params("parallel", "arbitrary"),
        name="peer_experts",
    )(h, e1, e2, gates, u, u, v, v)


def _peer(h, w_q, sub_keys, u, v):
    q = _mm(h, w_q.astype(BF16), BF16)
    e1, e2, gates = _peer_topk(q, sub_keys.astype(BF16))
    return _peer_experts(h, e1, e2, gates, u.astype(BF16), v.astype(BF16))


def kernel(x, c, ada_w, ada_b, ln_g, ln_b, fp_w_in, fp_w_fourier, fp_w_pool, fp_pool_scale, fp_w_out,
           at_w_in, at_q_norm, at_k_norm, at_rpb, at_w_out, peer_w_q, peer_sub_keys, peer_u, peer_v):
    batch, seq, d = x.shape
    depth = ada_w.shape[0]
    alpha = (2.0 * depth) ** 0.25
    xt = x.reshape(batch * seq, d)
    ada = _adaln_all(c, ada_w, ada_b)[:, :batch]
    shift = ada[:, :, None, :d]
    scale = ada[:, :, None, d:2 * d]
    gate = ada[:, :, None, 2 * d:]
    h = _modulate(xt, shift[0], scale[0], seq)
    for n in range(2 * depth):
        l, sub = divmod(n, 2)
        i = l // 2
        if sub == 1:
            y = _peer(h, peer_w_q[l], peer_sub_keys[l], peer_u[l], peer_v[l])
        elif l % 2 == 0:
            y = _fourier_pool_mixer(h, fp_w_in[i], fp_w_fourier[i], fp_w_pool[i], fp_pool_scale[i],
                                    fp_w_out[i], batch, seq)
        else:
            y = _attention_mixer(h, at_w_in[i], at_q_norm[i], at_k_norm[i], at_rpb[i], at_w_out[i],
                                 batch, seq)
        if n + 1 < 2 * depth:
            xt, h = _postnorm_modulate(xt, y, gate[n], ln_g[l, sub], ln_b[l, sub],
                                       shift[n + 1], scale[n + 1], seq, alpha)
        else:
            xt = _postnorm(xt, y, gate[n], ln_g[l, sub], ln_b[l, sub], seq, alpha)
    return xt.reshape(batch, seq, d)
```

```python
import functools
import math

import numpy as np
import jax
import jax.numpy as jnp
from jax import lax
from jax.experimental import pallas as pl
from jax.experimental.pallas import tpu as pltpu

F32 = jnp.float32
BF16 = jnp.bfloat16

GRID_COLS = 64
HEAD = 128
GROUP_W = 256
N_FOURIER = 4
N_POOL = 4
POOL_WINDOWS = (2, 4, 8, 16)
N_Q_HEADS = 8
N_KV_HEADS = 2
N_NA_HEADS = 8
NA_ROWS = 8
NA_COLS = 16
ROPE_THETA = 10000.0
PEER_HEADS = 8
PEER_KEYS = 128
PEER_TOPK = 16
LN_EPS = 1e-6
FFT_N1 = 128
FFT_K1_BLOCK = 8

VMEM_LIMIT_BYTES = 56 * 1024 * 1024
LANES = 128
NEG_BIG = -1e30


def _params(*sem):
    return pltpu.CompilerParams(dimension_semantics=sem, vmem_limit_bytes=VMEM_LIMIT_BYTES)


def _dot(a, b):
    return jnp.dot(a, b, preferred_element_type=F32)


def _dot_nt(a, b):
    return lax.dot_general(a, b, (((1,), (1,)), ((), ())), preferred_element_type=F32)


def _layer_norm(x):
    mu = jnp.mean(x, axis=-1, keepdims=True)
    xc = x - mu
    var = jnp.mean(xc * xc, axis=-1, keepdims=True)
    return xc * lax.rsqrt(var + LN_EPS)


def _adaln_kernel(c_ref, w_ref, b_ref, o_ref):
    c = c_ref[...].astype(BF16)
    w = w_ref[0].astype(BF16)
    o_ref[0] = _dot(c, w) + b_ref[0]


def _adaln_all(c, ada_w, ada_b):
    depth, two, d, d3 = ada_w.shape
    n = depth * two
    b = c.shape[0]
    c_pad = jnp.zeros((8, d), F32).at[:b].set(c)
    w = ada_w.reshape(n, d, d3)
    bias = ada_b.reshape(n, 1, d3)
    tn = 768
    return pl.pallas_call(
        _adaln_kernel,
        grid=(n, d3 // tn),
        in_specs=[pl.BlockSpec((8, d), lambda i, j: (0, 0)),
                  pl.BlockSpec((1, d, tn), lambda i, j: (i, 0, j)),
                  pl.BlockSpec((1, 1, tn), lambda i, j: (i, 0, j))],
        out_specs=pl.BlockSpec((1, 8, tn), lambda i, j: (i, 0, j)),
        out_shape=jax.ShapeDtypeStruct((n, 8, d3), F32),
        compiler_params=_params("parallel", "parallel"),
        name="adaln",
    )(c_pad, w, bias)


def _modulate_kernel(x_ref, sh_ref, sc_ref, o_ref):
    y = _layer_norm(x_ref[...])
    o_ref[...] = (y * (1.0 + sc_ref[0]) + sh_ref[0]).astype(o_ref.dtype)


def _modulate(x, shift, scale, seq):
    t, d = x.shape
    tm = 512
    per_b = seq // tm
    return pl.pallas_call(
        _modulate_kernel,
        grid=(t // tm,),
        in_specs=[pl.BlockSpec((tm, d), lambda i: (i, 0)),
                  pl.BlockSpec((1, 1, d), lambda i: (i // per_b, 0, 0)),
                  pl.BlockSpec((1, 1, d), lambda i: (i // per_b, 0, 0))],
        out_specs=pl.BlockSpec((tm, d), lambda i: (i, 0)),
        out_shape=jax.ShapeDtypeStruct((t, d), BF16),
        compiler_params=_params("parallel"),
        name="modulate",
    )(x, shift, scale)


def _postnorm_kernel(x_ref, y_ref, gate_ref, g_ref, b_ref, o_ref, *, alpha):
    u = alpha * x_ref[...] + (1.0 + gate_ref[0]) * y_ref[...]
    o_ref[...] = _layer_norm(u) * g_ref[...] + b_ref[...]


def _postnorm(x, y, gate, g, b, seq, alpha):
    t, d = x.shape
    tm = 512
    per_b = seq // tm
    return pl.pallas_call(
        functools.partial(_postnorm_kernel, alpha=alpha),
        grid=(t // tm,),
        in_specs=[pl.BlockSpec((tm, d), lambda i: (i, 0)),
                  pl.BlockSpec((tm, d), lambda i: (i, 0)),
                  pl.BlockSpec((1, 1, d), lambda i: (i // per_b, 0, 0)),
                  pl.BlockSpec((1, d), lambda i: (0, 0)),
                  pl.BlockSpec((1, d), lambda i: (0, 0))],
        out_specs=pl.BlockSpec((tm, d), lambda i: (i, 0)),
        out_shape=jax.ShapeDtypeStruct((t, d), F32),
        compiler_params=_params("parallel"),
        name="postnorm",
    )(x, y, gate, g.reshape(1, d), b.reshape(1, d))


def _postnorm_modulate_kernel(x_ref, y_ref, gate_ref, g_ref, b_ref, sh_ref, sc_ref, o_ref, h_ref, *, alpha):
    u = alpha * x_ref[...] + (1.0 + gate_ref[0]) * y_ref[...]
    xn = _layer_norm(u) * g_ref[...] + b_ref[...]
    o_ref[...] = xn
    h_ref[...] = (_layer_norm(xn) * (1.0 + sc_ref[0]) + sh_ref[0]).astype(h_ref.dtype)


def _postnorm_modulate(x, y, gate, g, b, shift, scale, seq, alpha):
    t, d = x.shape
    tm = 512
    per_b = seq // tm
    row = pl.BlockSpec((tm, d), lambda i: (i, 0))
    per_batch = pl.BlockSpec((1, 1, d), lambda i: (i // per_b, 0, 0))
    vec = pl.BlockSpec((1, d), lambda i: (0, 0))
    return pl.pallas_call(
        functools.partial(_postnorm_modulate_kernel, alpha=alpha),
        grid=(t // tm,),
        in_specs=[row, row, per_batch, vec, vec, per_batch, per_batch],
        out_specs=[row, row],
        out_shape=[jax.ShapeDtypeStruct((t, d), F32), jax.ShapeDtypeStruct((t, d), BF16)],
        compiler_params=_params("parallel"),
        name="postnorm_modulate",
    )(x, y, gate, g.reshape(1, d), b.reshape(1, d), shift, scale)


def _mm_kernel(a_ref, b_ref, o_ref):
    o_ref[...] = _dot(a_ref[...], b_ref[...]).astype(o_ref.dtype)


def _mm(a, w, out_dtype, tm=512, tn=1024):
    m, k = a.shape
    n = w.shape[1]
    tm = min(tm, m)
    tn = min(tn, n)
    return pl.pallas_call(
        _mm_kernel,
        grid=(n // tn, m // tm),
        in_specs=[pl.BlockSpec((tm, k), lambda j, i: (i, 0)),
                  pl.BlockSpec((k, tn), lambda j, i: (0, j))],
        out_specs=pl.BlockSpec((tm, tn), lambda j, i: (i, j)),
        out_shape=jax.ShapeDtypeStruct((m, n), out_dtype),
        compiler_params=_params("parallel", "parallel"),
        name="mm",
    )(a, w)


def _mm2_kernel(a1_ref, a2_ref, w1_ref, w2_ref, o_ref):
    acc = _dot(a1_ref[...].astype(BF16), w1_ref[...])
    acc = acc + _dot(a2_ref[...].astype(BF16), w2_ref[...])
    o_ref[...] = acc


def _mm2(a1, a2, w, tm=512, tn=1024):
    m, k = a1.shape
    n = w.shape[1]
    tm = min(tm, m)
    tn = min(tn, n)
    return pl.pallas_call(
        _mm2_kernel,
        grid=(n // tn, m // tm),
        in_specs=[pl.BlockSpec((tm, k), lambda j, i: (i, 0)),
                  pl.BlockSpec((tm, k), lambda j, i: (i, 0)),
                  pl.BlockSpec((k, tn), lambda j, i: (0, j)),
                  pl.BlockSpec((k, tn), lambda j, i: (1, j))],
        out_specs=pl.BlockSpec((tm, tn), lambda j, i: (i, j)),
        out_shape=jax.ShapeDtypeStruct((m, n), F32),
        compiler_params=_params("parallel", "parallel"),
        name="mm2",
    )(a1, a2, w, w)


def _fourier_proj_kernel(z_ref, w_ref, a_ref, b_ref):
    r = _dot(z_ref[...], w_ref[0])
    a_ref[...] = r[:, :GROUP_W].astype(a_ref.dtype)
    b_ref[...] = r[:, GROUP_W:].astype(b_ref.dtype)


def _fourier_proj(z, w_ab):
    t = z.shape[0]
    tm = 1024
    wid = N_FOURIER * GROUP_W
    return pl.pallas_call(
        _fourier_proj_kernel,
        grid=(N_FOURIER, t // tm),
        in_specs=[pl.BlockSpec((tm, GROUP_W), lambda g, i: (i, g)),
                  pl.BlockSpec((1, GROUP_W, 2 * GROUP_W), lambda g, i: (g, 0, 0))],
        out_specs=[pl.BlockSpec((tm, GROUP_W), lambda g, i: (i, g)),
                   pl.BlockSpec((tm, GROUP_W), lambda g, i: (i, g))],
        out_shape=[jax.ShapeDtypeStruct((t, wid), BF16), jax.ShapeDtypeStruct((t, wid), BF16)],
        compiler_params=_params("parallel", "parallel"),
        name="fourier_proj",
    )(z, w_ab)


def _fft1_kernel(a_ref, b_ref, c_ref, s_ref, twc_ref, tws_ref, tr_ref, ti_ref):
    a = a_ref[0]
    b = b_ref[0]
    c = c_ref[...]
    s = s_ref[...]
    tr = _dot(c, a) + _dot(s, b)
    ti = _dot(c, b) - _dot(s, a)
    reps = a.shape[1] // LANES
    wc = jnp.concatenate([twc_ref[0]] * reps, axis=1)
    ws = jnp.concatenate([tws_ref[0]] * reps, axis=1)
    tr_ref[0] = (tr * wc + ti * ws).astype(tr_ref.dtype)
    ti_ref[0] = (ti * wc - tr * ws).astype(ti_ref.dtype)


def _fft2_kernel(tr_ref, ti_ref, lr_ref, li_ref, o_ref):
    kb, n2, cols = tr_ref.shape[1:]
    tr = tr_ref[0].reshape(kb * n2, cols)
    ti = ti_ref[0].reshape(kb * n2, cols)
    res = _dot(lr_ref[...], tr) + _dot(li_ref[...], ti)
    o_ref[0] = res.reshape(n2, kb, cols)


def _seq_dft_real(a, b, batch, seq):
    wid = a.shape[1]
    n1 = FFT_N1
    n2 = seq // n1
    kb = FFT_K1_BLOCK
    k = np.arange(n1)
    ang1 = 2.0 * np.pi * np.outer(k, k) / n1
    c1 = jnp.asarray(np.cos(ang1), BF16)
    s1 = jnp.asarray(np.sin(ang1), BF16)
    angt = 2.0 * np.pi * np.outer(np.arange(n2), k) / seq
    twc = jnp.asarray(np.broadcast_to(np.cos(angt)[:, :, None], (n2, n1, LANES)), F32)
    tws = jnp.asarray(np.broadcast_to(np.sin(angt)[:, :, None], (n2, n1, LANES)), F32)
    a3 = a.reshape(batch, n1, n2 * wid)
    b3 = b.reshape(batch, n1, n2 * wid)
    tr, ti = pl.pallas_call(
        _fft1_kernel,
        grid=(batch, n2),
        in_specs=[pl.BlockSpec((1, n1, wid), lambda bb, j: (bb, 0, j)),
                  pl.BlockSpec((1, n1, wid), lambda bb, j: (bb, 0, j)),
                  pl.BlockSpec((n1, n1), lambda bb, j: (0, 0)),
                  pl.BlockSpec((n1, n1), lambda bb, j: (0, 0)),
                  pl.BlockSpec((1, n1, LANES), lambda bb, j: (j, 0, 0)),
                  pl.BlockSpec((1, n1, LANES), lambda bb, j: (j, 0, 0))],
        out_specs=[pl.BlockSpec((1, n1, wid), lambda bb, j: (bb, 0, j)),
                   pl.BlockSpec((1, n1, wid), lambda bb, j: (bb, 0, j))],
        out_shape=[jax.ShapeDtypeStruct((batch, n1, n2 * wid), BF16)] * 2,
        compiler_params=_params("parallel", "parallel"),
        name="fft_stage1",
    )(a3, b3, c1, s1, twc, tws)
    k2 = np.arange(n2)
    ang2 = 2.0 * np.pi * np.outer(k2, k2) / n2
    eye = np.eye(kb)
    lr = np.einsum('kn,jJ->kjJn', np.cos(ang2), eye).reshape(n2 * kb, kb * n2)
    li = np.einsum('kn,jJ->kjJn', np.sin(ang2), eye).reshape(n2 * kb, kb * n2)
    tr4 = tr.reshape(batch, n1, n2, wid)
    ti4 = ti.reshape(batch, n1, n2, wid)
    out = pl.pallas_call(
        _fft2_kernel,
        grid=(batch, n1 // kb),
        in_specs=[pl.BlockSpec((1, kb, n2, wid), lambda bb, j: (bb, j, 0, 0)),
                  pl.BlockSpec((1, kb, n2, wid), lambda bb, j: (bb, j, 0, 0)),
                  pl.BlockSpec((n2 * kb, kb * n2), lambda bb, j: (0, 0)),
                  pl.BlockSpec((n2 * kb, kb * n2), lambda bb, j: (0, 0))],
        out_specs=pl.BlockSpec((1, n2, kb, wid), lambda bb, j: (bb, 0, j, 0)),
        out_shape=jax.ShapeDtypeStruct((batch, n2, n1, wid), F32),
        compiler_params=_params("parallel", "parallel"),
        name="fft_stage2",
    )(tr4, ti4, jnp.asarray(lr, BF16), jnp.asarray(li, BF16))
    return out.reshape(batch * seq, wid)


def _fourier_weights(w_fourier, seq):
    g, cg, _ = w_fourier.shape
    k = np.arange(cg)
    ang = 2.0 * np.pi * np.outer(k, k) / cg
    norm = 1.0 / math.sqrt(seq * cg)
    cs = jnp.asarray(np.concatenate([np.cos(ang), -np.sin(ang)], axis=0) * norm, BF16)
    wf = jnp.transpose(w_fourier, (1, 0, 2)).reshape(cg, g * cg).astype(BF16)
    r = _mm(cs, wf, F32)
    r = r.reshape(2, cg, g, cg)
    return jnp.concatenate([r[0], r[1]], axis=-1).transpose(1, 0, 2).astype(BF16)


POOL_TM = 256
POOL_PAD = 128


def _pool_kernel(z_ref, band_ref, w_ref, ps_ref, o_ref, zp_ref, *, win, seq):
    cg = z_ref.shape[2]
    zp_ref[0:POOL_PAD, :] = jnp.zeros((POOL_PAD, cg), BF16)
    zp_ref[seq + POOL_PAD:seq + 2 * POOL_PAD, :] = jnp.zeros((POOL_PAD, cg), BF16)
    zp_ref[POOL_PAD:seq + POOL_PAD, :] = z_ref[0]
    band = band_ref[...]
    w = w_ref[0]
    ps = ps_ref[0]
    half = win // 2

    def body(i, carry):
        t0 = pl.multiple_of(i * POOL_TM, POOL_TM)
        slab = zp_ref[pl.ds(t0, POOL_TM + 2 * POOL_PAD), :]
        sums = _dot(band, slab)
        t = t0 + lax.broadcasted_iota(jnp.int32, (POOL_TM, cg), 0)
        cnt = (jnp.minimum(t + half, seq) - jnp.maximum(t - half, 0)).astype(F32)
        zc = zp_ref[pl.ds(t0 + POOL_PAD, POOL_TM), :].astype(F32)
        p = sums / cnt - zc
        y = _dot(p.astype(BF16), w) * ps
        o_ref[0, pl.ds(t0, POOL_TM), :] = y.astype(o_ref.dtype)
        return carry

    lax.fori_loop(0, seq // POOL_TM, body, 0)


def _pool_group(z3, w_pool, pool_scale, g, batch, seq):
    win = POOL_WINDOWS[g]
    half = win // 2
    tau = np.arange(POOL_TM)[:, None]
    kap = np.arange(POOL_TM + 2 * POOL_PAD)[None, :]
    off = kap - POOL_PAD - tau
    band = jnp.asarray(((off >= -half) & (off < half)).astype(np.float32), BF16)
    col = N_FOURIER + g
    return pl.pallas_call(
        functools.partial(_pool_kernel, win=win, seq=seq),
        grid=(batch,),
        in_specs=[pl.BlockSpec((1, seq, GROUP_W), lambda bb: (bb, 0, col)),
                  pl.BlockSpec((POOL_TM, POOL_TM + 2 * POOL_PAD), lambda bb: (0, 0)),
                  pl.BlockSpec((1, GROUP_W, GROUP_W), lambda bb: (g, 0, 0)),
                  pl.BlockSpec((1, 1, GROUP_W), lambda bb: (g, 0, 0))],
        out_specs=pl.BlockSpec((1, seq, GROUP_W), lambda bb: (bb, 0, 0)),
        out_shape=jax.ShapeDtypeStruct((batch, seq, GROUP_W), BF16),
        scratch_shapes=[pltpu.VMEM((seq + 2 * POOL_PAD, GROUP_W), BF16)],
        compiler_params=_params("parallel"),
        name=f"pool_w{win}",
    )(z3, band, w_pool, pool_scale)


def _fourier_pool_mixer(h, w_in, w_fourier, w_pool, pool_scale, w_out, batch, seq):
    t = h.shape[0]
    z = _mm(h, w_in.astype(BF16), BF16)
    a, b = _fourier_proj(z, _fourier_weights(w_fourier, seq))
    yf = _seq_dft_real(a, b, batch, seq)
    z3 = z.reshape(batch, seq, z.shape[1])
    wp = w_pool.astype(BF16)
    ps = pool_scale.reshape(N_POOL, 1, GROUP_W)
    yp = jnp.concatenate([_pool_group(z3, wp, ps, g, batch, seq) for g in range(N_POOL)], axis=-1)
    return _mm2(yf, yp.reshape(t, N_POOL * GROUP_W), w_out.astype(BF16))


def _qk_prep_kernel(x_ref, g_ref, cos_ref, sin_ref, o_ref):
    x = x_ref[...].astype(F32)
    y = x * lax.rsqrt(jnp.mean(x * x, axis=-1, keepdims=True) + LN_EPS) * g_ref[0]
    lane = lax.broadcasted_iota(jnp.int32, y.shape, 1)
    first = (lane % (HEAD // 2)) < (HEAD // 4)
    partner = jnp.where(first, pltpu.roll(y, HEAD - HEAD // 4, 1), pltpu.roll(y, HEAD // 4, 1))
    o_ref[...] = (y * cos_ref[...] + partner * sin_ref[...]).astype(o_ref.dtype)


def _rope_tables(seq):
    quarter = HEAD // 4
    inv = ROPE_THETA ** (-np.arange(quarter, dtype=np.float64) / quarter)
    t = np.arange(seq)
    ang_r = (t // GRID_COLS)[:, None] * inv[None, :]
    ang_c = (t % GRID_COLS)[:, None] * inv[None, :]
    cos = np.concatenate([np.cos(ang_r), np.cos(ang_r), np.cos(ang_c), np.cos(ang_c)], axis=1)
    sin = np.concatenate([-np.sin(ang_r), np.sin(ang_r), -np.sin(ang_c), np.sin(ang_c)], axis=1)
    return jnp.asarray(cos, F32), jnp.asarray(sin, F32)


def _qk_prep(proj, gains, seq):
    t = proj.shape[0]
    nh = gains.shape[0]
    tm = 1024
    per_b = seq // tm
    cos, sin = _rope_tables(seq)
    return pl.pallas_call(
        _qk_prep_kernel,
        grid=(t // tm, nh),
        in_specs=[pl.BlockSpec((tm, HEAD), lambda i, hh: (i, hh)),
                  pl.BlockSpec((1, 1, HEAD), lambda i, hh: (hh, 0, 0)),
                  pl.BlockSpec((tm, HEAD), lambda i, hh: (i % per_b, 0)),
                  pl.BlockSpec((tm, HEAD), lambda i, hh: (i % per_b, 0))],
        out_specs=pl.BlockSpec((tm, HEAD), lambda i, hh: (i, hh)),
        out_shape=jax.ShapeDtypeStruct((t, nh * HEAD), BF16),
        compiler_params=_params("parallel", "parallel"),
        name="qk_prep",
    )(proj, gains, cos, sin)


FLASH_TQ = 512
FLASH_TK = 1024
FLASH_UNROLL = 4


def _flash_kernel(q_ref, k_ref, v_ref, o_ref):
    seq = k_ref.shape[0]
    tq = q_ref.shape[0]
    for hh in range(N_Q_HEADS // N_KV_HEADS):
        q = q_ref[:, hh * HEAD:(hh + 1) * HEAD]

        def body(c, carry):
            m, l, acc = carry
            c0 = pl.multiple_of(c * FLASH_TK, FLASH_TK)
            k = k_ref[pl.ds(c0, FLASH_TK), :]
            v = v_ref[pl.ds(c0, FLASH_TK), :]
            s = _dot_nt(q, k)
            m_new = jnp.maximum(m, jnp.max(s, axis=-1, keepdims=True))
            p = jnp.exp2(s - m_new)
            corr = jnp.exp2(m - m_new)
            l = l * corr + jnp.sum(p, axis=-1, keepdims=True)
            acc = acc * corr + _dot(p.astype(BF16), v)
            return m_new, l, acc

        init = (jnp.full((tq, 1), NEG_BIG, F32), jnp.zeros((tq, 1), F32), jnp.zeros((tq, HEAD), F32))
        m, l, acc = lax.fori_loop(0, seq // FLASH_TK, body, init, unroll=FLASH_UNROLL)
        o_ref[:, hh * HEAD:(hh + 1) * HEAD] = (acc / l).astype(o_ref.dtype)


def _gqa_attention(qk, proj, batch, seq):
    t = qk.shape[0]
    per_b = seq // FLASH_TQ
    gw = (N_Q_HEADS // N_KV_HEADS) * HEAD
    k_col = N_Q_HEADS
    v_col = N_Q_HEADS + N_KV_HEADS
    return pl.pallas_call(
        _flash_kernel,
        grid=(batch, N_KV_HEADS, per_b),
        in_specs=[pl.BlockSpec((FLASH_TQ, gw), lambda bb, kv, i: (bb * per_b + i, kv)),
                  pl.BlockSpec((seq, HEAD), lambda bb, kv, i: (bb, k_col + kv)),
                  pl.BlockSpec((seq, HEAD), lambda bb, kv, i: (bb, v_col + kv))],
        out_specs=pl.BlockSpec((FLASH_TQ, gw), lambda bb, kv, i: (bb * per_b + i, kv)),
        out_shape=jax.ShapeDtypeStruct((t, N_Q_HEADS * HEAD), BF16),
        compiler_params=_params("parallel", "parallel", "parallel"),
        name="gqa_flash",
    )(qk, qk, proj)


NA_ROW_BLOCK = 8


def _na_kernel(q_ref, k_ref, v_ref, bias_ref, o_ref, *, rows, scale):
    i = pl.program_id(2)
    span = NA_ROWS * GRID_COLS
    starts, scores = [], []
    for rr in range(NA_ROW_BLOCK):
        r = i * NA_ROW_BLOCK + rr
        rs = jnp.clip(r - NA_ROWS // 2, 0, rows - NA_ROWS)
        dr0 = rs - r + (NA_ROWS - 1)
        k0 = pl.multiple_of(rs * GRID_COLS, GRID_COLS)
        q = q_ref[rr * GRID_COLS:(rr + 1) * GRID_COLS, :]
        starts.append(k0)
        scores.append(_dot_nt(q, k_ref[pl.ds(k0, span), :]) * scale + bias_ref[0, dr0])
    probs = []
    for s in scores:
        p = jnp.exp(s - jnp.max(s, axis=-1, keepdims=True))
        probs.append((p / jnp.sum(p, axis=-1, keepdims=True)).astype(BF16))
    outs = [_dot(p, v_ref[pl.ds(k0, span), :]) for p, k0 in zip(probs, starts)]
    o_ref[...] = jnp.concatenate(outs, axis=0).astype(o_ref.dtype)


def _na_bias_table(rpb):
    cols = np.arange(GRID_COLS)
    start = np.clip(cols - NA_COLS // 2, 0, GRID_COLS - NA_COLS)
    kc = np.arange(GRID_COLS)
    valid = (kc[None, :] >= start[:, None]) & (kc[None, :] < start[:, None] + NA_COLS)
    off = np.clip(kc[None, :] - cols[:, None] + (NA_COLS - 1), 0, 2 * NA_COLS - 2)
    tab = rpb[:, :, off]
    tab = jnp.where(jnp.asarray(valid)[None, None], tab, NEG_BIG)
    per_dr0 = [jnp.concatenate([tab[:, d + j] for j in range(NA_ROWS)], axis=-1) for d in range(NA_ROWS)]
    return jnp.stack(per_dr0, axis=1).astype(F32)


def _neighbourhood_attention(proj, rpb, batch, seq):
    t = proj.shape[0]
    rows = seq // GRID_COLS
    qb = NA_ROW_BLOCK * GRID_COLS
    per_b = seq // qb
    q_col = (N_Q_HEADS + 2 * N_KV_HEADS)
    k_col = q_col + N_NA_HEADS
    v_col = k_col + N_NA_HEADS
    bias = _na_bias_table(rpb)
    return pl.pallas_call(
        functools.partial(_na_kernel, rows=rows, scale=HEAD ** -0.5),
        grid=(batch, N_NA_HEADS, per_b),
        in_specs=[pl.BlockSpec((qb, HEAD), lambda bb, hh, i: (bb * per_b + i, q_col + hh)),
                  pl.BlockSpec((seq, HEAD), lambda bb, hh, i: (bb, k_col + hh)),
                  pl.BlockSpec((seq, HEAD), lambda bb, hh, i: (bb, v_col + hh)),
                  pl.BlockSpec((1, NA_ROWS, GRID_COLS, NA_ROWS * GRID_COLS), lambda bb, hh, i: (hh, 0, 0, 0))],
        out_specs=pl.BlockSpec((qb, HEAD), lambda bb, hh, i: (bb * per_b + i, hh)),
        out_shape=jax.ShapeDtypeStruct((t, N_NA_HEADS * HEAD), BF16),
        compiler_params=_params("parallel", "parallel", "parallel"),
        name="natten",
    )(proj, proj, proj, bias)


def _attention_mixer(h, w_in, q_norm, k_norm, rpb, w_out, batch, seq):
    proj = _mm(h, w_in.astype(BF16), BF16, tn=1152)
    q_gain = HEAD ** -0.5 * math.log2(math.e)
    gains = jnp.concatenate([jnp.tile(q_norm[None] * q_gain, (N_Q_HEADS, 1)),
                             jnp.tile(k_norm[None], (N_KV_HEADS, 1))], axis=0)
    qk = _qk_prep(proj, gains.reshape(N_Q_HEADS + N_KV_HEADS, 1, HEAD).astype(F32), seq)
    yc = _gqa_attention(qk, proj, batch, seq)
    yd = _neighbourhood_attention(proj, rpb, batch, seq)
    return _mm2(yc, yd, w_out.astype(BF16))


def _row(v, i, fill):
    rows = lax.broadcasted_iota(jnp.int32, v.shape, 0)
    return jnp.max(jnp.where(rows == i, v, fill), axis=0, keepdims=True)


def _top16(s, ids, big):
    rows, n = s.shape
    half = rows // 2
    id_a, id_b = ids
    by_id = jnp.where(id_a < id_b, 1.0, 0.0)
    first = jnp.where(s[:half] > s[half:], 1.0, jnp.where(s[:half] < s[half:], 0.0, by_id)) > 0.5
    top = jnp.where(first, s[:half], s[half:])
    low = jnp.where(first, s[half:], s[:half])
    top_id = jnp.where(first, id_a, id_b)
    low_id = jnp.where(first, id_b, id_a)
    slot = lax.broadcasted_iota(jnp.int32, (PEER_TOPK, n), 0)
    vals = jnp.zeros((PEER_TOPK, n), F32)
    idxs = jnp.zeros((PEER_TOPK, n), F32)
    for it in range(PEER_TOPK):
        m = jnp.max(top, axis=0, keepdims=True)
        ix = jnp.min(jnp.where(top == m, top_id, big), axis=0, keepdims=True)
        hit = top_id == ix
        top = jnp.where(hit, low, top)
        top_id = jnp.where(hit, low_id, top_id)
        low = jnp.where(hit, -jnp.inf, low)
        vals = jnp.where(slot == it, m, vals)
        idxs = jnp.where(slot == it, ix, idxs)
    return vals, idxs


def _pick(table, sel):
    out = jnp.zeros_like(table)
    for r in range(PEER_TOPK):
        out = jnp.where(sel == r, _row(table, r, -1.0), out)
    return out


def _peer_topk_kernel(q_ref, keys_ref, e1_ref, e2_ref, g_ref):
    tm = q_ref.shape[0]
    half = q_ref.shape[1] // 2
    k = PEER_TOPK
    s1 = _dot_nt(keys_ref[0, 0], q_ref[:, :half])
    s2 = _dot_nt(keys_ref[0, 1], q_ref[:, half:])
    key_lo = lax.broadcasted_iota(jnp.int32, (PEER_KEYS // 2, tm), 0).astype(F32)
    key_ids = (key_lo, key_lo + float(PEER_KEYS // 2))
    v1, i1 = _top16(s1, key_ids, float(PEER_KEYS))
    v2, i2 = _top16(s2, key_ids, float(PEER_KEYS))
    i8 = lax.broadcasted_iota(jnp.int32, (8, tm), 0)
    i16 = lax.broadcasted_iota(jnp.int32, (k, tm), 0)
    ninf = -jnp.inf
    cand = [_row(v1, 0, ninf) + v2]
    ids = [i16]
    for i in (1, 2, 3):
        cand.append(_row(v1, i, ninf) + v2[0:8])
        ids.append(i * k + i8)
    cand.append(v1[8:16] + _row(v2, 0, ninf))
    ids.append((i8 + 8) * k)
    for j in (0, 1):
        cand.append(jnp.where(i8 < 4, ninf, v1[0:8] + _row(v2, j, ninf)))
        ids.append(i8 * k + j)
    cand = jnp.concatenate(cand, axis=0)
    ids = (jnp.concatenate(ids[:3], axis=0).astype(F32), jnp.concatenate(ids[3:], axis=0).astype(F32))
    sc, cf = _top16(cand, ids, float(k * k))
    ci = cf.astype(jnp.int32)
    ex = jnp.exp(sc - _row(sc, 0, ninf))
    e1_ref[0] = _pick(i1, lax.shift_right_logical(ci, 4))
    e2_ref[0] = _pick(i2, lax.bitwise_and(ci, k - 1))
    g_ref[0] = ex / jnp.sum(ex, axis=0, keepdims=True)


def _peer_topk(q, sub_keys):
    t = q.shape[0]
    tm = 256
    qd = q.shape[1] // PEER_HEADS
    out = jax.ShapeDtypeStruct((PEER_HEADS, PEER_TOPK, t), F32)
    spec = pl.BlockSpec((1, PEER_TOPK, tm), lambda i, hh: (hh, 0, i))
    return pl.pallas_call(
        _peer_topk_kernel,
        grid=(t // tm, PEER_HEADS),
        in_specs=[pl.BlockSpec((tm, qd), lambda i, hh: (i, hh)),
                  pl.BlockSpec((1, 2, PEER_KEYS, qd // 2), lambda i, hh: (hh, 0, 0, 0))],
        out_specs=[spec, spec, spec],
        out_shape=[out, out, out],
        compiler_params=_params("parallel", "parallel"),
        name="peer_topk",
    )(q, sub_keys)


PEER_TM = 512
PEER_SUB = 512
PEER_BUILD_UNROLL = 32
PEER_ROW_PAD = 8
HI16 = 0xFFFF0000


def _peer_experts_kernel(h_ref, e1_ref, e2_ref, gt_ref, ulo_ref, uhi_ref, vlo_ref, vhi_ref, o_ref,
                         gmat, e1s, e2s, gs):
    c = pl.program_id(1)
    tm = h_ref.shape[0]
    slots = PEER_HEADS * PEER_TOPK
    pitch = gmat.shape[0] // tm
    half = pitch - PEER_ROW_PAD

    @pl.when(c == 0)
    def _():
        o_ref[...] = jnp.zeros_like(o_ref)
        e1s[...] = e1_ref[...].reshape(slots, tm).T
        e2s[...] = e2_ref[...].reshape(slots, tm).T
        gs[...] = gt_ref[...].reshape(slots, tm).T
        key = lax.broadcasted_iota(jnp.int32, (PEER_KEYS, slots), 0).astype(F32)

        def body(t, carry):
            r1 = e1s[pl.ds(t, 1), :]
            r2 = e2s[pl.ds(t, 1), :]
            rg = gs[pl.ds(t, 1), :]
            a = jnp.where(key == r1, 1.0, 0.0).astype(BF16)
            b = jnp.where(key == r2, rg, 0.0).astype(BF16)
            g = _dot_nt(a, b).astype(BF16).astype(F32)
            hi = pltpu.bitcast(g[:half], jnp.uint32)
            lo = pltpu.bitcast(g[half:], jnp.uint32)
            gmat[pl.ds(pl.multiple_of(t * pitch, PEER_ROW_PAD), half), :] = hi | (lo >> 16)
            return carry

        lax.fori_loop(0, tm, body, 0, unroll=PEER_BUILD_UNROLL)

    per_sub = ulo_ref.shape[0] // PEER_KEYS
    words = [gmat[pl.ds(c * per_sub + j, tm, stride=pitch), :] for j in range(per_sub)]
    g_lo = jnp.concatenate([pltpu.bitcast(w & jnp.uint32(HI16), F32) for w in words], axis=1)
    g_hi = jnp.concatenate([pltpu.bitcast(w << 16, F32) for w in words], axis=1)
    for u_ref, v_ref, g in ((ulo_ref, vlo_ref, g_lo), (uhi_ref, vhi_ref, g_hi)):
        a = _dot_nt(h_ref[...], u_ref[...])
        act = 0.5 * a * (1.0 + lax.erf(a * (1.0 / math.sqrt(2.0))))
        o_ref[...] += _dot((act * g).astype(BF16), v_ref[...])


def _peer_experts(h, e1, e2, gates, u, v, layer):
    t, d = h.shape
    n_exp = v.shape[1]
    tm = min(PEER_TM, t)
    half = PEER_KEYS // 2
    steps = n_exp // 2 // PEER_SUB
    slots = PEER_HEADS * PEER_TOPK
    sel = pl.BlockSpec((PEER_HEADS, PEER_TOPK, tm), lambda i, c: (0, 0, i))
    lo_blk = pl.BlockSpec((None, PEER_SUB, d), lambda i, c: (layer, c, 0))
    hi_blk = pl.BlockSpec((None, PEER_SUB, d), lambda i, c: (layer, steps + c, 0))
    return pl.pallas_call(
        _peer_experts_kernel,
        grid=(t // tm, steps),
        in_specs=[pl.BlockSpec((tm, d), lambda i, c: (i, 0)), sel, sel, sel, lo_blk, hi_blk, lo_blk, hi_blk],
        out_specs=pl.BlockSpec((tm, d), lambda i, c: (i, 0)),
        out_shape=jax.ShapeDtypeStruct((t, d), F32),
        scratch_shapes=[pltpu.VMEM((tm * (half + PEER_ROW_PAD), PEER_KEYS), jnp.uint32)]
        + [pltpu.VMEM((tm, slots), F32)] * 3,
        compiler_params=_params("parallel", "arbitrary"),
        name="peer_experts",
    )(h, e1, e2, gates, u, u, v, v)


def _peer(h, w_q, sub_keys, u_all, v_all, layer):
    q = _mm(h, w_q.astype(BF16), BF16)
    e1, e2, gates = _peer_topk(q, sub_keys.astype(BF16))
    return _peer_experts(h, e1, e2, gates, u_all, v_all, layer)


def kernel(x, c, ada_w, ada_b, ln_g, ln_b, fp_w_in, fp_w_fourier, fp_w_pool, fp_pool_scale, fp_w_out,
           at_w_in, at_q_norm, at_k_norm, at_rpb, at_w_out, peer_w_q, peer_sub_keys, peer_u, peer_v):
    batch, seq, d = x.shape
    depth = ada_w.shape[0]
    alpha = (2.0 * depth) ** 0.25
    xt = x.reshape(batch * seq, d)
    ada = _adaln_all(c, ada_w, ada_b)[:, :batch]
    shift = ada[:, :, None, :d]
    scale = ada[:, :, None, d:2 * d]
    gate = ada[:, :, None, 2 * d:]
    u_all = peer_u.astype(BF16)
    v_all = peer_v.astype(BF16)
    h = _modulate(xt, shift[0], scale[0], seq)
    for n in range(2 * depth):
        l, sub = divmod(n, 2)
        i = l // 2
        if sub == 1:
            y = _peer(h, peer_w_q[l], peer_sub_keys[l], u_all, v_all, l)
        elif l % 2 == 0:
            y = _fourier_pool_mixer(h, fp_w_in[i], fp_w_fourier[i], fp_w_pool[i], fp_pool_scale[i],
                                    fp_w_out[i], batch, seq)
        else:
            y = _attention_mixer(h, at_w_in[i], at_q_norm[i], at_k_norm[i], at_rpb[i], at_w_out[i],
                                 batch, seq)
        if n + 1 < 2 * depth:
            xt, h = _postnorm_modulate(xt, y, gate[n], ln_g[l, sub], ln_b[l, sub],
                                       shift[n + 1], scale[n + 1], seq, alpha)
        else:
            xt = _postnorm(xt, y, gate[n], ln_g[l, sub], ln_b[l, sub], seq, alpha)
    return xt.reshape(batch, seq, d)
```

```python
import functools
import math

import numpy as np
import jax
import jax.numpy as jnp
from jax import lax
from jax.experimental import pallas as pl
from jax.experimental.pallas import tpu as pltpu

F32 = jnp.float32
BF16 = jnp.bfloat16

GRID_COLS = 64
HEAD = 128
GROUP_W = 256
N_FOURIER = 4
N_POOL = 4
POOL_WINDOWS = (2, 4, 8, 16)
N_Q_HEADS = 8
N_KV_HEADS = 2
N_NA_HEADS = 8
NA_ROWS = 8
NA_COLS = 16
ROPE_THETA = 10000.0
PEER_HEADS = 8
PEER_KEYS = 128
PEER_TOPK = 16
LN_EPS = 1e-6
FFT_N1 = 128
FFT_K1_BLOCK = 8

VMEM_LIMIT_BYTES = 56 * 1024 * 1024
LANES = 128
NEG_BIG = -1e30


def _params(*sem):
    return pltpu.CompilerParams(dimension_semantics=sem, vmem_limit_bytes=VMEM_LIMIT_BYTES)


def _dot(a, b):
    return jnp.dot(a, b, preferred_element_type=F32)


def _dot_nt(a, b):
    return lax.dot_general(a, b, (((1,), (1,)), ((), ())), preferred_element_type=F32)


def _layer_norm(x):
    mu = jnp.mean(x, axis=-1, keepdims=True)
    xc = x - mu
    var = jnp.mean(xc * xc, axis=-1, keepdims=True)
    return xc * lax.rsqrt(var + LN_EPS)


def _adaln_kernel(c_ref, w_ref, b_ref, o_ref):
    c = c_ref[...].astype(BF16)
    w = w_ref[0].astype(BF16)
    o_ref[0] = _dot(c, w) + b_ref[0]


def _adaln_all(c, ada_w, ada_b):
    depth, two, d, d3 = ada_w.shape
    n = depth * two
    b = c.shape[0]
    c_pad = jnp.zeros((8, d), F32).at[:b].set(c)
    w = ada_w.reshape(n, d, d3)
    bias = ada_b.reshape(n, 1, d3)
    tn = 768
    return pl.pallas_call(
        _adaln_kernel,
        grid=(n, d3 // tn),
        in_specs=[pl.BlockSpec((8, d), lambda i, j: (0, 0)),
                  pl.BlockSpec((1, d, tn), lambda i, j: (i, 0, j)),
                  pl.BlockSpec((1, 1, tn), lambda i, j: (i, 0, j))],
        out_specs=pl.BlockSpec((1, 8, tn), lambda i, j: (i, 0, j)),
        out_shape=jax.ShapeDtypeStruct((n, 8, d3), F32),
        compiler_params=_params("parallel", "parallel"),
        name="adaln",
    )(c_pad, w, bias)


def _modulate_kernel(x_ref, sh_ref, sc_ref, o_ref):
    y = _layer_norm(x_ref[...])
    o_ref[...] = (y * (1.0 + sc_ref[0]) + sh_ref[0]).astype(o_ref.dtype)


def _modulate(x, shift, scale, seq):
    t, d = x.shape
    tm = 512
    per_b = seq // tm
    return pl.pallas_call(
        _modulate_kernel,
        grid=(t // tm,),
        in_specs=[pl.BlockSpec((tm, d), lambda i: (i, 0)),
                  pl.BlockSpec((1, 1, d), lambda i: (i // per_b, 0, 0)),
                  pl.BlockSpec((1, 1, d), lambda i: (i // per_b, 0, 0))],
        out_specs=pl.BlockSpec((tm, d), lambda i: (i, 0)),
        out_shape=jax.ShapeDtypeStruct((t, d), BF16),
        compiler_params=_params("parallel"),
        name="modulate",
    )(x, shift, scale)


def _postnorm_kernel(x_ref, y_ref, gate_ref, g_ref, b_ref, o_ref, *, alpha):
    u = alpha * x_ref[...] + (1.0 + gate_ref[0]) * y_ref[...]
    o_ref[...] = _layer_norm(u) * g_ref[...] + b_ref[...]


def _postnorm(x, y, gate, g, b, seq, alpha):
    t, d = x.shape
    tm = 512
    per_b = seq // tm
    return pl.pallas_call(
        functools.partial(_postnorm_kernel, alpha=alpha),
        grid=(t // tm,),
        in_specs=[pl.BlockSpec((tm, d), lambda i: (i, 0)),
                  pl.BlockSpec((tm, d), lambda i: (i, 0)),
                  pl.BlockSpec((1, 1, d), lambda i: (i // per_b, 0, 0)),
                  pl.BlockSpec((1, d), lambda i: (0, 0)),
                  pl.BlockSpec((1, d), lambda i: (0, 0))],
        out_specs=pl.BlockSpec((tm, d), lambda i: (i, 0)),
        out_shape=jax.ShapeDtypeStruct((t, d), F32),
        compiler_params=_params("parallel"),
        name="postnorm",
    )(x, y, gate, g.reshape(1, d), b.reshape(1, d))


def _postnorm_modulate_kernel(x_ref, y_ref, gate_ref, g_ref, b_ref, sh_ref, sc_ref, o_ref, h_ref, *, alpha):
    u = alpha * x_ref[...] + (1.0 + gate_ref[0]) * y_ref[...]
    xn = _layer_norm(u) * g_ref[...] + b_ref[...]
    o_ref[...] = xn
    h_ref[...] = (_layer_norm(xn) * (1.0 + sc_ref[0]) + sh_ref[0]).astype(h_ref.dtype)


def _postnorm_modulate(x, y, gate, g, b, shift, scale, seq, alpha):
    t, d = x.shape
    tm = 512
    per_b = seq // tm
    row = pl.BlockSpec((tm, d), lambda i: (i, 0))
    per_batch = pl.BlockSpec((1, 1, d), lambda i: (i // per_b, 0, 0))
    vec = pl.BlockSpec((1, d), lambda i: (0, 0))
    return pl.pallas_call(
        functools.partial(_postnorm_modulate_kernel, alpha=alpha),
        grid=(t // tm,),
        in_specs=[row, row, per_batch, vec, vec, per_batch, per_batch],
        out_specs=[row, row],
        out_shape=[jax.ShapeDtypeStruct((t, d), F32), jax.ShapeDtypeStruct((t, d), BF16)],
        compiler_params=_params("parallel"),
        name="postnorm_modulate",
    )(x, y, gate, g.reshape(1, d), b.reshape(1, d), shift, scale)


def _mm_kernel(a_ref, b_ref, o_ref):
    o_ref[...] = _dot(a_ref[...], b_ref[...]).astype(o_ref.dtype)


def _mm(a, w, out_dtype, tm=1024, tn=1024):
    m, k = a.shape
    n = w.shape[1]
    tm = min(tm, m)
    tn = min(tn, n)
    return pl.pallas_call(
        _mm_kernel,
        grid=(n // tn, m // tm),
        in_specs=[pl.BlockSpec((tm, k), lambda j, i: (i, 0)),
                  pl.BlockSpec((k, tn), lambda j, i: (0, j))],
        out_specs=pl.BlockSpec((tm, tn), lambda j, i: (i, j)),
        out_shape=jax.ShapeDtypeStruct((m, n), out_dtype),
        compiler_params=_params("parallel", "parallel"),
        name="mm",
    )(a, w)


def _mm2_kernel(a1_ref, a2_ref, w1_ref, w2_ref, o_ref):
    acc = _dot(a1_ref[...].astype(BF16), w1_ref[...])
    acc = acc + _dot(a2_ref[...].astype(BF16), w2_ref[...])
    o_ref[...] = acc


def _mm2(a1, a2, w, tm=1024, tn=1024):
    m, k = a1.shape
    n = w.shape[1]
    tm = min(tm, m)
    tn = min(tn, n)
    return pl.pallas_call(
        _mm2_kernel,
        grid=(n // tn, m // tm),
        in_specs=[pl.BlockSpec((tm, k), lambda j, i: (i, 0)),
                  pl.BlockSpec((tm, k), lambda j, i: (i, 0)),
                  pl.BlockSpec((k, tn), lambda j, i: (0, j)),
                  pl.BlockSpec((k, tn), lambda j, i: (1, j))],
        out_specs=pl.BlockSpec((tm, tn), lambda j, i: (i, j)),
        out_shape=jax.ShapeDtypeStruct((m, n), F32),
        compiler_params=_params("parallel", "parallel"),
        name="mm2",
    )(a1, a2, w, w)


def _fourier_proj_kernel(z_ref, w_ref, a_ref, b_ref):
    r = _dot(z_ref[...], w_ref[0])
    a_ref[...] = r[:, :GROUP_W].astype(a_ref.dtype)
    b_ref[...] = r[:, GROUP_W:].astype(b_ref.dtype)


def _fourier_proj(z, w_ab):
    t = z.shape[0]
    tm = 1024
    wid = N_FOURIER * GROUP_W
    return pl.pallas_call(
        _fourier_proj_kernel,
        grid=(N_FOURIER, t // tm),
        in_specs=[pl.BlockSpec((tm, GROUP_W), lambda g, i: (i, g)),
                  pl.BlockSpec((1, GROUP_W, 2 * GROUP_W), lambda g, i: (g, 0, 0))],
        out_specs=[pl.BlockSpec((tm, GROUP_W), lambda g, i: (i, g)),
                   pl.BlockSpec((tm, GROUP_W), lambda g, i: (i, g))],
        out_shape=[jax.ShapeDtypeStruct((t, wid), BF16), jax.ShapeDtypeStruct((t, wid), BF16)],
        compiler_params=_params("parallel", "parallel"),
        name="fourier_proj",
    )(z, w_ab)


def _fft1_kernel(a_ref, b_ref, c_ref, s_ref, twc_ref, tws_ref, tr_ref, ti_ref):
    a = a_ref[0]
    b = b_ref[0]
    c = c_ref[...]
    s = s_ref[...]
    tr = _dot(c, a) + _dot(s, b)
    ti = _dot(c, b) - _dot(s, a)
    reps = a.shape[1] // LANES
    wc = jnp.concatenate([twc_ref[0]] * reps, axis=1)
    ws = jnp.concatenate([tws_ref[0]] * reps, axis=1)
    tr_ref[0] = (tr * wc + ti * ws).astype(tr_ref.dtype)
    ti_ref[0] = (ti * wc - tr * ws).astype(ti_ref.dtype)


def _fft2_kernel(tr_ref, ti_ref, lr_ref, li_ref, o_ref):
    kb, n2, cols = tr_ref.shape[1:]
    tr = tr_ref[0].reshape(kb * n2, cols)
    ti = ti_ref[0].reshape(kb * n2, cols)
    res = _dot(lr_ref[...], tr) + _dot(li_ref[...], ti)
    o_ref[0] = res.reshape(n2, kb, cols)


def _seq_dft_real(a, b, batch, seq):
    wid = a.shape[1]
    n1 = FFT_N1
    n2 = seq // n1
    kb = FFT_K1_BLOCK
    k = np.arange(n1)
    ang1 = 2.0 * np.pi * np.outer(k, k) / n1
    c1 = jnp.asarray(np.cos(ang1), BF16)
    s1 = jnp.asarray(np.sin(ang1), BF16)
    angt = 2.0 * np.pi * np.outer(np.arange(n2), k) / seq
    twc = jnp.asarray(np.broadcast_to(np.cos(angt)[:, :, None], (n2, n1, LANES)), F32)
    tws = jnp.asarray(np.broadcast_to(np.sin(angt)[:, :, None], (n2, n1, LANES)), F32)
    a3 = a.reshape(batch, n1, n2 * wid)
    b3 = b.reshape(batch, n1, n2 * wid)
    tr, ti = pl.pallas_call(
        _fft1_kernel,
        grid=(batch, n2),
        in_specs=[pl.BlockSpec((1, n1, wid), lambda bb, j: (bb, 0, j)),
                  pl.BlockSpec((1, n1, wid), lambda bb, j: (bb, 0, j)),
                  pl.BlockSpec((n1, n1), lambda bb, j: (0, 0)),
                  pl.BlockSpec((n1, n1), lambda bb, j: (0, 0)),
                  pl.BlockSpec((1, n1, LANES), lambda bb, j: (j, 0, 0)),
                  pl.BlockSpec((1, n1, LANES), lambda bb, j: (j, 0, 0))],
        out_specs=[pl.BlockSpec((1, n1, wid), lambda bb, j: (bb, 0, j)),
                   pl.BlockSpec((1, n1, wid), lambda bb, j: (bb, 0, j))],
        out_shape=[jax.ShapeDtypeStruct((batch, n1, n2 * wid), BF16)] * 2,
        compiler_params=_params("parallel", "parallel"),
        name="fft_stage1",
    )(a3, b3, c1, s1, twc, tws)
    k2 = np.arange(n2)
    ang2 = 2.0 * np.pi * np.outer(k2, k2) / n2
    eye = np.eye(kb)
    lr = np.einsum('kn,jJ->kjJn', np.cos(ang2), eye).reshape(n2 * kb, kb * n2)
    li = np.einsum('kn,jJ->kjJn', np.sin(ang2), eye).reshape(n2 * kb, kb * n2)
    tr4 = tr.reshape(batch, n1, n2, wid)
    ti4 = ti.reshape(batch, n1, n2, wid)
    out = pl.pallas_call(
        _fft2_kernel,
        grid=(batch, n1 // kb),
        in_specs=[pl.BlockSpec((1, kb, n2, wid), lambda bb, j: (bb, j, 0, 0)),
                  pl.BlockSpec((1, kb, n2, wid), lambda bb, j: (bb, j, 0, 0)),
                  pl.BlockSpec((n2 * kb, kb * n2), lambda bb, j: (0, 0)),
                  pl.BlockSpec((n2 * kb, kb * n2), lambda bb, j: (0, 0))],
        out_specs=pl.BlockSpec((1, n2, kb, wid), lambda bb, j: (bb, 0, j, 0)),
        out_shape=jax.ShapeDtypeStruct((batch, n2, n1, wid), F32),
        compiler_params=_params("parallel", "parallel"),
        name="fft_stage2",
    )(tr4, ti4, jnp.asarray(lr, BF16), jnp.asarray(li, BF16))
    return out.reshape(batch * seq, wid)


def _fourier_weights(w_fourier, seq):
    g, cg, _ = w_fourier.shape
    k = np.arange(cg)
    ang = 2.0 * np.pi * np.outer(k, k) / cg
    norm = 1.0 / math.sqrt(seq * cg)
    cs = jnp.asarray(np.concatenate([np.cos(ang), -np.sin(ang)], axis=0) * norm, BF16)
    wf = jnp.transpose(w_fourier, (1, 0, 2)).reshape(cg, g * cg).astype(BF16)
    r = _mm(cs, wf, F32)
    r = r.reshape(2, cg, g, cg)
    return jnp.concatenate([r[0], r[1]], axis=-1).transpose(1, 0, 2).astype(BF16)


POOL_TM = 256
POOL_PAD = 128


def _pool_kernel(z_ref, band_ref, w_ref, ps_ref, o_ref, zp_ref, *, win, seq):
    cg = z_ref.shape[2]
    zp_ref[0:POOL_PAD, :] = jnp.zeros((POOL_PAD, cg), BF16)
    zp_ref[seq + POOL_PAD:seq + 2 * POOL_PAD, :] = jnp.zeros((POOL_PAD, cg), BF16)
    zp_ref[POOL_PAD:seq + POOL_PAD, :] = z_ref[0]
    band = band_ref[...]
    w = w_ref[0]
    ps = ps_ref[0]
    half = win // 2

    def body(i, carry):
        t0 = pl.multiple_of(i * POOL_TM, POOL_TM)
        slab = zp_ref[pl.ds(t0, POOL_TM + 2 * POOL_PAD), :]
        sums = _dot(band, slab)
        t = t0 + lax.broadcasted_iota(jnp.int32, (POOL_TM, cg), 0)
        cnt = (jnp.minimum(t + half, seq) - jnp.maximum(t - half, 0)).astype(F32)
        zc = zp_ref[pl.ds(t0 + POOL_PAD, POOL_TM), :].astype(F32)
        p = sums / cnt - zc
        y = _dot(p.astype(BF16), w) * ps
        o_ref[0, pl.ds(t0, POOL_TM), :] = y.astype(o_ref.dtype)
        return carry

    lax.fori_loop(0, seq // POOL_TM, body, 0)


def _pool_group(z3, w_pool, pool_scale, g, batch, seq):
    win = POOL_WINDOWS[g]
    half = win // 2
    tau = np.arange(POOL_TM)[:, None]
    kap = np.arange(POOL_TM + 2 * POOL_PAD)[None, :]
    off = kap - POOL_PAD - tau
    band = jnp.asarray(((off >= -half) & (off < half)).astype(np.float32), BF16)
    col = N_FOURIER + g
    return pl.pallas_call(
        functools.partial(_pool_kernel, win=win, seq=seq),
        grid=(batch,),
        in_specs=[pl.BlockSpec((1, seq, GROUP_W), lambda bb: (bb, 0, col)),
                  pl.BlockSpec((POOL_TM, POOL_TM + 2 * POOL_PAD), lambda bb: (0, 0)),
                  pl.BlockSpec((1, GROUP_W, GROUP_W), lambda bb: (g, 0, 0)),
                  pl.BlockSpec((1, 1, GROUP_W), lambda bb: (g, 0, 0))],
        out_specs=pl.BlockSpec((1, seq, GROUP_W), lambda bb: (bb, 0, 0)),
        out_shape=jax.ShapeDtypeStruct((batch, seq, GROUP_W), BF16),
        scratch_shapes=[pltpu.VMEM((seq + 2 * POOL_PAD, GROUP_W), BF16)],
        compiler_params=_params("parallel"),
        name=f"pool_w{win}",
    )(z3, band, w_pool, pool_scale)


def _fourier_pool_mixer(h, w_in, w_fourier, w_pool, pool_scale, w_out, batch, seq):
    t = h.shape[0]
    z = _mm(h, w_in.astype(BF16), BF16)
    a, b = _fourier_proj(z, _fourier_weights(w_fourier, seq))
    yf = _seq_dft_real(a, b, batch, seq)
    z3 = z.reshape(batch, seq, z.shape[1])
    wp = w_pool.astype(BF16)
    ps = pool_scale.reshape(N_POOL, 1, GROUP_W)
    yp = jnp.concatenate([_pool_group(z3, wp, ps, g, batch, seq) for g in range(N_POOL)], axis=-1)
    return _mm2(yf, yp.reshape(t, N_POOL * GROUP_W), w_out.astype(BF16))


def _qk_prep_kernel(x_ref, g_ref, cos_ref, sin_ref, o_ref):
    x = x_ref[...].astype(F32)
    y = x * lax.rsqrt(jnp.mean(x * x, axis=-1, keepdims=True) + LN_EPS) * g_ref[0]
    lane = lax.broadcasted_iota(jnp.int32, y.shape, 1)
    first = (lane % (HEAD // 2)) < (HEAD // 4)
    partner = jnp.where(first, pltpu.roll(y, HEAD - HEAD // 4, 1), pltpu.roll(y, HEAD // 4, 1))
    o_ref[...] = (y * cos_ref[...] + partner * sin_ref[...]).astype(o_ref.dtype)


def _rope_tables(seq):
    quarter = HEAD // 4
    inv = ROPE_THETA ** (-np.arange(quarter, dtype=np.float64) / quarter)
    t = np.arange(seq)
    ang_r = (t // GRID_COLS)[:, None] * inv[None, :]
    ang_c = (t % GRID_COLS)[:, None] * inv[None, :]
    cos = np.concatenate([np.cos(ang_r), np.cos(ang_r), np.cos(ang_c), np.cos(ang_c)], axis=1)
    sin = np.concatenate([-np.sin(ang_r), np.sin(ang_r), -np.sin(ang_c), np.sin(ang_c)], axis=1)
    return jnp.asarray(cos, F32), jnp.asarray(sin, F32)


def _qk_prep(proj, gains, seq):
    t = proj.shape[0]
    nh = gains.shape[0]
    tm = 1024
    per_b = seq // tm
    cos, sin = _rope_tables(seq)
    return pl.pallas_call(
        _qk_prep_kernel,
        grid=(t // tm, nh),
        in_specs=[pl.BlockSpec((tm, HEAD), lambda i, hh: (i, hh)),
                  pl.BlockSpec((1, 1, HEAD), lambda i, hh: (hh, 0, 0)),
                  pl.BlockSpec((tm, HEAD), lambda i, hh: (i % per_b, 0)),
                  pl.BlockSpec((tm, HEAD), lambda i, hh: (i % per_b, 0))],
        out_specs=pl.BlockSpec((tm, HEAD), lambda i, hh: (i, hh)),
        out_shape=jax.ShapeDtypeStruct((t, nh * HEAD), BF16),
        compiler_params=_params("parallel", "parallel"),
        name="qk_prep",
    )(proj, gains, cos, sin)


FLASH_TQ = 512
FLASH_TK = 1024
FLASH_UNROLL = 4


def _flash_kernel(q_ref, k_ref, v_ref, o_ref):
    seq = k_ref.shape[0]
    tq = q_ref.shape[0]
    for hh in range(N_Q_HEADS // N_KV_HEADS):
        q = q_ref[:, hh * HEAD:(hh + 1) * HEAD]

        def body(c, carry):
            m, l, acc = carry
            c0 = pl.multiple_of(c * FLASH_TK, FLASH_TK)
            k = k_ref[pl.ds(c0, FLASH_TK), :]
            v = v_ref[pl.ds(c0, FLASH_TK), :]
            s = _dot_nt(q, k)
            m_new = jnp.maximum(m, jnp.max(s, axis=-1, keepdims=True))
            p = jnp.exp2(s - m_new)
            corr = jnp.exp2(m - m_new)
            l = l * corr + jnp.sum(p, axis=-1, keepdims=True)
            acc = acc * corr + _dot(p.astype(BF16), v)
            return m_new, l, acc

        init = (jnp.full((tq, 1), NEG_BIG, F32), jnp.zeros((tq, 1), F32), jnp.zeros((tq, HEAD), F32))
        m, l, acc = lax.fori_loop(0, seq // FLASH_TK, body, init, unroll=FLASH_UNROLL)
        o_ref[:, hh * HEAD:(hh + 1) * HEAD] = (acc / l).astype(o_ref.dtype)


def _gqa_attention(qk, proj, batch, seq):
    t = qk.shape[0]
    per_b = seq // FLASH_TQ
    gw = (N_Q_HEADS // N_KV_HEADS) * HEAD
    k_col = N_Q_HEADS
    v_col = N_Q_HEADS + N_KV_HEADS
    return pl.pallas_call(
        _flash_kernel,
        grid=(batch, N_KV_HEADS, per_b),
        in_specs=[pl.BlockSpec((FLASH_TQ, gw), lambda bb, kv, i: (bb * per_b + i, kv)),
                  pl.BlockSpec((seq, HEAD), lambda bb, kv, i: (bb, k_col + kv)),
                  pl.BlockSpec((seq, HEAD), lambda bb, kv, i: (bb, v_col + kv))],
        out_specs=pl.BlockSpec((FLASH_TQ, gw), lambda bb, kv, i: (bb * per_b + i, kv)),
        out_shape=jax.ShapeDtypeStruct((t, N_Q_HEADS * HEAD), BF16),
        compiler_params=_params("parallel", "parallel", "parallel"),
        name="gqa_flash",
    )(qk, qk, proj)


NA_ROW_BLOCK = 8


def _na_kernel(q_ref, k_ref, v_ref, bias_ref, o_ref, *, rows, scale):
    i = pl.program_id(2)
    span = NA_ROWS * GRID_COLS
    starts, scores = [], []
    for rr in range(NA_ROW_BLOCK):
        r = i * NA_ROW_BLOCK + rr
        rs = jnp.clip(r - NA_ROWS // 2, 0, rows - NA_ROWS)
        dr0 = rs - r + (NA_ROWS - 1)
        k0 = pl.multiple_of(rs * GRID_COLS, GRID_COLS)
        q = q_ref[rr * GRID_COLS:(rr + 1) * GRID_COLS, :]
        starts.append(k0)
        scores.append(_dot_nt(q, k_ref[pl.ds(k0, span), :]) * scale + bias_ref[0, dr0])
    probs = []
    for s in scores:
        p = jnp.exp(s - jnp.max(s, axis=-1, keepdims=True))
        probs.append((p / jnp.sum(p, axis=-1, keepdims=True)).astype(BF16))
    outs = [_dot(p, v_ref[pl.ds(k0, span), :]) for p, k0 in zip(probs, starts)]
    o_ref[...] = jnp.concatenate(outs, axis=0).astype(o_ref.dtype)


def _na_bias_table(rpb):
    cols = np.arange(GRID_COLS)
    start = np.clip(cols - NA_COLS // 2, 0, GRID_COLS - NA_COLS)
    kc = np.arange(GRID_COLS)
    valid = (kc[None, :] >= start[:, None]) & (kc[None, :] < start[:, None] + NA_COLS)
    off = np.clip(kc[None, :] - cols[:, None] + (NA_COLS - 1), 0, 2 * NA_COLS - 2)
    tab = rpb[:, :, off]
    tab = jnp.where(jnp.asarray(valid)[None, None], tab, NEG_BIG)
    per_dr0 = [jnp.concatenate([tab[:, d + j] for j in range(NA_ROWS)], axis=-1) for d in range(NA_ROWS)]
    return jnp.stack(per_dr0, axis=1).astype(F32)


def _neighbourhood_attention(proj, rpb, batch, seq):
    t = proj.shape[0]
    rows = seq // GRID_COLS
    qb = NA_ROW_BLOCK * GRID_COLS
    per_b = seq // qb
    q_col = (N_Q_HEADS + 2 * N_KV_HEADS)
    k_col = q_col + N_NA_HEADS
    v_col = k_col + N_NA_HEADS
    bias = _na_bias_table(rpb)
    return pl.pallas_call(
        functools.partial(_na_kernel, rows=rows, scale=HEAD ** -0.5),
        grid=(batch, N_NA_HEADS, per_b),
        in_specs=[pl.BlockSpec((qb, HEAD), lambda bb, hh, i: (bb * per_b + i, q_col + hh)),
                  pl.BlockSpec((seq, HEAD), lambda bb, hh, i: (bb, k_col + hh)),
                  pl.BlockSpec((seq, HEAD), lambda bb, hh, i: (bb, v_col + hh)),
                  pl.BlockSpec((1, NA_ROWS, GRID_COLS, NA_ROWS * GRID_COLS), lambda bb, hh, i: (hh, 0, 0, 0))],
        out_specs=pl.BlockSpec((qb, HEAD), lambda bb, hh, i: (bb * per_b + i, hh)),
        out_shape=jax.ShapeDtypeStruct((t, N_NA_HEADS * HEAD), BF16),
        compiler_params=_params("parallel", "parallel", "parallel"),
        name="natten",
    )(proj, proj, proj, bias)


def _attention_mixer(h, w_in, q_norm, k_norm, rpb, w_out, batch, seq):
    proj = _mm(h, w_in.astype(BF16), BF16, tn=1152)
    q_gain = HEAD ** -0.5 * math.log2(math.e)
    gains = jnp.concatenate([jnp.tile(q_norm[None] * q_gain, (N_Q_HEADS, 1)),
                             jnp.tile(k_norm[None], (N_KV_HEADS, 1))], axis=0)
    qk = _qk_prep(proj, gains.reshape(N_Q_HEADS + N_KV_HEADS, 1, HEAD).astype(F32), seq)
    yc = _gqa_attention(qk, proj, batch, seq)
    yd = _neighbourhood_attention(proj, rpb, batch, seq)
    return _mm2(yc, yd, w_out.astype(BF16))


def _row(v, i, fill):
    rows = lax.broadcasted_iota(jnp.int32, v.shape, 0)
    return jnp.max(jnp.where(rows == i, v, fill), axis=0, keepdims=True)


def _top16(s, ids, big):
    rows, n = s.shape
    quarter = rows // 4
    lst = [s[i * quarter:(i + 1) * quarter] for i in range(4)]
    lid = list(ids)

    def order(i, j):
        a, b, ia, ib = lst[i], lst[j], lid[i], lid[j]
        swap = jnp.where(b > a, 1.0, jnp.where(b < a, 0.0, jnp.where(ib < ia, 1.0, 0.0))) > 0.5
        lst[i], lst[j] = jnp.where(swap, b, a), jnp.where(swap, a, b)
        lid[i], lid[j] = jnp.where(swap, ib, ia), jnp.where(swap, ia, ib)

    for i, j in ((0, 1), (2, 3), (0, 2), (1, 3), (1, 2)):
        order(i, j)
    slot = lax.broadcasted_iota(jnp.int32, (PEER_TOPK, n), 0)
    vals = jnp.zeros((PEER_TOPK, n), F32)
    idxs = jnp.zeros((PEER_TOPK, n), F32)
    for it in range(PEER_TOPK):
        m = jnp.max(lst[0], axis=0, keepdims=True)
        ix = jnp.min(jnp.where(lst[0] == m, lid[0], big), axis=0, keepdims=True)
        hit = lid[0] == ix
        for k in range(3):
            lst[k] = jnp.where(hit, lst[k + 1], lst[k])
            lid[k] = jnp.where(hit, lid[k + 1], lid[k])
        lst[3] = jnp.where(hit, -jnp.inf, lst[3])
        vals = jnp.where(slot == it, m, vals)
        idxs = jnp.where(slot == it, ix, idxs)
    return vals, idxs


def _pick(table, sel):
    out = jnp.zeros_like(table)
    for r in range(PEER_TOPK):
        out = jnp.where(sel == r, _row(table, r, -1.0), out)
    return out


def _peer_topk_kernel(q_ref, keys_ref, e1_ref, e2_ref, g_ref):
    tm = q_ref.shape[0]
    half = q_ref.shape[1] // 2
    k = PEER_TOPK
    s1 = _dot_nt(keys_ref[0, 0], q_ref[:, :half])
    s2 = _dot_nt(keys_ref[0, 1], q_ref[:, half:])
    key_lo = lax.broadcasted_iota(jnp.int32, (PEER_KEYS // 4, tm), 0).astype(F32)
    key_ids = tuple(key_lo + float(i * (PEER_KEYS // 4)) for i in range(4))
    v1, i1 = _top16(s1, key_ids, float(PEER_KEYS))
    v2, i2 = _top16(s2, key_ids, float(PEER_KEYS))
    i8 = lax.broadcasted_iota(jnp.int32, (8, tm), 0)
    i16 = lax.broadcasted_iota(jnp.int32, (k, tm), 0)
    ninf = -jnp.inf
    cand = [_row(v1, 0, ninf) + v2]
    ids = [i16]
    for i in (1, 2, 3):
        cand.append(_row(v1, i, ninf) + v2[0:8])
        ids.append(i * k + i8)
    cand.append(v1[8:16] + _row(v2, 0, ninf))
    ids.append((i8 + 8) * k)
    for j in (0, 1):
        cand.append(jnp.where(i8 < 4, ninf, v1[0:8] + _row(v2, j, ninf)))
        ids.append(i8 * k + j)
    cand = jnp.concatenate(cand, axis=0)
    ids = (ids[0].astype(F32),) + tuple(jnp.concatenate(ids[i:i + 2], axis=0).astype(F32) for i in (1, 3, 5))
    sc, cf = _top16(cand, ids, float(k * k))
    ci = cf.astype(jnp.int32)
    ex = jnp.exp(sc - _row(sc, 0, ninf))
    e1_ref[0] = _pick(i1, lax.shift_right_logical(ci, 4))
    e2_ref[0] = _pick(i2, lax.bitwise_and(ci, k - 1))
    g_ref[0] = ex / jnp.sum(ex, axis=0, keepdims=True)


def _peer_topk(q, sub_keys):
    t = q.shape[0]
    tm = 512
    qd = q.shape[1] // PEER_HEADS
    out = jax.ShapeDtypeStruct((PEER_HEADS, PEER_TOPK, t), F32)
    spec = pl.BlockSpec((1, PEER_TOPK, tm), lambda i, hh: (hh, 0, i))
    return pl.pallas_call(
        _peer_topk_kernel,
        grid=(t // tm, PEER_HEADS),
        in_specs=[pl.BlockSpec((tm, qd), lambda i, hh: (i, hh)),
                  pl.BlockSpec((1, 2, PEER_KEYS, qd // 2), lambda i, hh: (hh, 0, 0, 0))],
        out_specs=[spec, spec, spec],
        out_shape=[out, out, out],
        compiler_params=_params("parallel", "parallel"),
        name="peer_topk",
    )(q, sub_keys)


PEER_TM = 512
PEER_SUB = 512
PEER_BUILD_UNROLL = 32
PEER_ROW_PAD = 8
HI16 = 0xFFFF0000


def _peer_experts_kernel(h_ref, e1_ref, e2_ref, gt_ref, ulo_ref, uhi_ref, vlo_ref, vhi_ref, o_ref,
                         gmat, e1s, e2s, gs):
    c = pl.program_id(1)
    tm = h_ref.shape[0]
    slots = PEER_HEADS * PEER_TOPK
    pitch = gmat.shape[0] // tm
    half = pitch - PEER_ROW_PAD

    @pl.when(c == 0)
    def _():
        o_ref[...] = jnp.zeros_like(o_ref)
        e1s[...] = e1_ref[...].reshape(slots, tm).T
        e2s[...] = e2_ref[...].reshape(slots, tm).T
        gs[...] = gt_ref[...].reshape(slots, tm).T
        key = lax.broadcasted_iota(jnp.int32, (PEER_KEYS, slots), 0).astype(F32)

        def body(t, carry):
            r1 = e1s[pl.ds(t, 1), :]
            r2 = e2s[pl.ds(t, 1), :]
            rg = gs[pl.ds(t, 1), :]
            a = jnp.where(key == r1, 1.0, 0.0).astype(BF16)
            b = jnp.where(key == r2, rg, 0.0).astype(BF16)
            g = _dot_nt(a, b).astype(BF16).astype(F32)
            hi = pltpu.bitcast(g[:half], jnp.uint32)
            lo = pltpu.bitcast(g[half:], jnp.uint32)
            gmat[pl.ds(pl.multiple_of(t * pitch, PEER_ROW_PAD), half), :] = hi | (lo >> 16)
            return carry

        lax.fori_loop(0, tm, body, 0, unroll=PEER_BUILD_UNROLL)

    per_sub = ulo_ref.shape[0] // PEER_KEYS
    words = [gmat[pl.ds(c * per_sub + j, tm, stride=pitch), :] for j in range(per_sub)]
    g_lo = jnp.concatenate([pltpu.bitcast(w & jnp.uint32(HI16), F32) for w in words], axis=1)
    g_hi = jnp.concatenate([pltpu.bitcast(w << 16, F32) for w in words], axis=1)
    for u_ref, v_ref, g in ((ulo_ref, vlo_ref, g_lo), (uhi_ref, vhi_ref, g_hi)):
        a = _dot_nt(h_ref[...], u_ref[...])
        act = 0.5 * a * (1.0 + lax.erf(a * (1.0 / math.sqrt(2.0))))
        o_ref[...] += _dot((act * g).astype(BF16), v_ref[...])


def _peer_experts(h, e1, e2, gates, u, v, layer):
    t, d = h.shape
    n_exp = v.shape[1]
    tm = min(PEER_TM, t)
    half = PEER_KEYS // 2
    steps = n_exp // 2 // PEER_SUB
    slots = PEER_HEADS * PEER_TOPK
    sel = pl.BlockSpec((PEER_HEADS, PEER_TOPK, tm), lambda i, c: (0, 0, i))
    lo_blk = pl.BlockSpec((None, PEER_SUB, d), lambda i, c: (layer, c, 0))
    hi_blk = pl.BlockSpec((None, PEER_SUB, d), lambda i, c: (layer, steps + c, 0))
    return pl.pallas_call(
        _peer_experts_kernel,
        grid=(t // tm, steps),
        in_specs=[pl.BlockSpec((tm, d), lambda i, c: (i, 0)), sel, sel, sel, lo_blk, hi_blk, lo_blk, hi_blk],
        out_specs=pl.BlockSpec((tm, d), lambda i, c: (i, 0)),
        out_shape=jax.ShapeDtypeStruct((t, d), F32),
        scratch_shapes=[pltpu.VMEM((tm * (half + PEER_ROW_PAD), PEER_KEYS), jnp.uint32)]
        + [pltpu.VMEM((tm, slots), F32)] * 3,
        compiler_params=_params("parallel", "arbitrary"),
        name="peer_experts",
    )(h, e1, e2, gates, u, u, v, v)


def _peer(h, w_q, sub_keys, u_all, v_all, layer):
    q = _mm(h, w_q.astype(BF16), BF16)
    e1, e2, gates = _peer_topk(q, sub_keys.astype(BF16))
    return _peer_experts(h, e1, e2, gates, u_all, v_all, layer)


def kernel(x, c, ada_w, ada_b, ln_g, ln_b, fp_w_in, fp_w_fourier, fp_w_pool, fp_pool_scale, fp_w_out,
           at_w_in, at_q_norm, at_k_norm, at_rpb, at_w_out, peer_w_q, peer_sub_keys, peer_u, peer_v):
    batch, seq, d = x.shape
    depth = ada_w.shape[0]
    alpha = (2.0 * depth) ** 0.25
    xt = x.reshape(batch * seq, d)
    ada = _adaln_all(c, ada_w, ada_b)[:, :batch]
    shift = ada[:, :, None, :d]
    scale = ada[:, :, None, d:2 * d]
    gate = ada[:, :, None, 2 * d:]
    u_all = peer_u.astype(BF16)
    v_all = peer_v.astype(BF16)
    h = _modulate(xt, shift[0], scale[0], seq)
    for n in range(2 * depth):
        l, sub = divmod(n, 2)
        i = l // 2
        if sub == 1:
            y = _peer(h, peer_w_q[l], peer_sub_keys[l], u_all, v_all, l)
        elif l % 2 == 0:
            y = _fourier_pool_mixer(h, fp_w_in[i], fp_w_fourier[i], fp_w_pool[i], fp_pool_scale[i],
                                    fp_w_out[i], batch, seq)
        else:
            y = _attention_mixer(h, at_w_in[i], at_q_norm[i], at_k_norm[i], at_rpb[i], at_w_out[i],
                                 batch, seq)
        if n + 1 < 2 * depth:
            xt, h = _postnorm_modulate(xt, y, gate[n], ln_g[l, sub], ln_b[l, sub],
                                       shift[n + 1], scale[n + 1], seq, alpha)
        else:
            xt = _postnorm(xt, y, gate[n], ln_g[l, sub], ln_b[l, sub], seq, alpha)
    return xt.reshape(batch, seq, d)
```

```python
import functools
import math

import numpy as np
import jax
import jax.numpy as jnp
from jax import lax
from jax.experimental import pallas as pl
from jax.experimental.pallas import tpu as pltpu

F32 = jnp.float32
BF16 = jnp.bfloat16

GRID_COLS = 64
HEAD = 128
GROUP_W = 256
N_FOURIER = 4
N_POOL = 4
POOL_WINDOWS = (2, 4, 8, 16)
N_Q_HEADS = 8
N_KV_HEADS = 2
N_NA_HEADS = 8
NA_ROWS = 8
NA_COLS = 16
ROPE_THETA = 10000.0
PEER_HEADS = 8
PEER_KEYS = 128
PEER_TOPK = 16
LN_EPS = 1e-6
FFT_N1 = 128
FFT_K1_BLOCK = 8

VMEM_LIMIT_BYTES = 56 * 1024 * 1024
LANES = 128
NEG_BIG = -1e30


def _params(*sem):
    return pltpu.CompilerParams(dimension_semantics=sem, vmem_limit_bytes=VMEM_LIMIT_BYTES)


def _dot(a, b):
    return jnp.dot(a, b, preferred_element_type=F32)


def _dot_nt(a, b):
    return lax.dot_general(a, b, (((1,), (1,)), ((), ())), preferred_element_type=F32)


def _layer_norm(x):
    mu = jnp.mean(x, axis=-1, keepdims=True)
    xc = x - mu
    var = jnp.mean(xc * xc, axis=-1, keepdims=True)
    return xc * lax.rsqrt(var + LN_EPS)


def _adaln_kernel(c_ref, w_ref, b_ref, o_ref):
    c = c_ref[...].astype(BF16)
    w = w_ref[0].astype(BF16)
    o_ref[0] = _dot(c, w) + b_ref[0]


def _adaln_all(c, ada_w, ada_b):
    depth, two, d, d3 = ada_w.shape
    n = depth * two
    b = c.shape[0]
    c_pad = jnp.zeros((8, d), F32).at[:b].set(c)
    w = ada_w.reshape(n, d, d3)
    bias = ada_b.reshape(n, 1, d3)
    tn = 768
    return pl.pallas_call(
        _adaln_kernel,
        grid=(n, d3 // tn),
        in_specs=[pl.BlockSpec((8, d), lambda i, j: (0, 0)),
                  pl.BlockSpec((1, d, tn), lambda i, j: (i, 0, j)),
                  pl.BlockSpec((1, 1, tn), lambda i, j: (i, 0, j))],
        out_specs=pl.BlockSpec((1, 8, tn), lambda i, j: (i, 0, j)),
        out_shape=jax.ShapeDtypeStruct((n, 8, d3), F32),
        compiler_params=_params("parallel", "parallel"),
        name="adaln",
    )(c_pad, w, bias)


def _modulate_kernel(x_ref, sh_ref, sc_ref, o_ref):
    y = _layer_norm(x_ref[...])
    o_ref[...] = (y * (1.0 + sc_ref[0]) + sh_ref[0]).astype(o_ref.dtype)


def _modulate(x, shift, scale, seq):
    t, d = x.shape
    tm = 512
    per_b = seq // tm
    return pl.pallas_call(
        _modulate_kernel,
        grid=(t // tm,),
        in_specs=[pl.BlockSpec((tm, d), lambda i: (i, 0)),
                  pl.BlockSpec((1, 1, d), lambda i: (i // per_b, 0, 0)),
                  pl.BlockSpec((1, 1, d), lambda i: (i // per_b, 0, 0))],
        out_specs=pl.BlockSpec((tm, d), lambda i: (i, 0)),
        out_shape=jax.ShapeDtypeStruct((t, d), BF16),
        compiler_params=_params("parallel"),
        name="modulate",
    )(x, shift, scale)


def _postnorm_kernel(x_ref, y_ref, gate_ref, g_ref, b_ref, o_ref, *, alpha):
    u = alpha * x_ref[...] + (1.0 + gate_ref[0]) * y_ref[...]
    o_ref[...] = _layer_norm(u) * g_ref[...] + b_ref[...]


def _postnorm(x, y, gate, g, b, seq, alpha):
    t, d = x.shape
    tm = 512
    per_b = seq // tm
    return pl.pallas_call(
        functools.partial(_postnorm_kernel, alpha=alpha),
        grid=(t // tm,),
        in_specs=[pl.BlockSpec((tm, d), lambda i: (i, 0)),
                  pl.BlockSpec((tm, d), lambda i: (i, 0)),
                  pl.BlockSpec((1, 1, d), lambda i: (i // per_b, 0, 0)),
                  pl.BlockSpec((1, d), lambda i: (0, 0)),
                  pl.BlockSpec((1, d), lambda i: (0, 0))],
        out_specs=pl.BlockSpec((tm, d), lambda i: (i, 0)),
        out_shape=jax.ShapeDtypeStruct((t, d), F32),
        compiler_params=_params("parallel"),
        name="postnorm",
    )(x, y, gate, g.reshape(1, d), b.reshape(1, d))


def _postnorm_modulate_kernel(x_ref, y_ref, gate_ref, g_ref, b_ref, sh_ref, sc_ref, o_ref, h_ref, *, alpha):
    u = alpha * x_ref[...] + (1.0 + gate_ref[0]) * y_ref[...]
    xn = _layer_norm(u) * g_ref[...] + b_ref[...]
    o_ref[...] = xn
    h_ref[...] = (_layer_norm(xn) * (1.0 + sc_ref[0]) + sh_ref[0]).astype(h_ref.dtype)


def _postnorm_modulate(x, y, gate, g, b, shift, scale, seq, alpha):
    t, d = x.shape
    tm = 512
    per_b = seq // tm
    row = pl.BlockSpec((tm, d), lambda i: (i, 0))
    per_batch = pl.BlockSpec((1, 1, d), lambda i: (i // per_b, 0, 0))
    vec = pl.BlockSpec((1, d), lambda i: (0, 0))
    return pl.pallas_call(
        functools.partial(_postnorm_modulate_kernel, alpha=alpha),
        grid=(t // tm,),
        in_specs=[row, row, per_batch, vec, vec, per_batch, per_batch],
        out_specs=[row, row],
        out_shape=[jax.ShapeDtypeStruct((t, d), F32), jax.ShapeDtypeStruct((t, d), BF16)],
        compiler_params=_params("parallel"),
        name="postnorm_modulate",
    )(x, y, gate, g.reshape(1, d), b.reshape(1, d), shift, scale)


def _mm_kernel(a_ref, b_ref, o_ref):
    o_ref[...] = _dot(a_ref[...], b_ref[...]).astype(o_ref.dtype)


def _mm(a, w, out_dtype, tm=1024, tn=1024):
    m, k = a.shape
    n = w.shape[1]
    tm = min(tm, m)
    tn = min(tn, n)
    return pl.pallas_call(
        _mm_kernel,
        grid=(n // tn, m // tm),
        in_specs=[pl.BlockSpec((tm, k), lambda j, i: (i, 0)),
                  pl.BlockSpec((k, tn), lambda j, i: (0, j))],
        out_specs=pl.BlockSpec((tm, tn), lambda j, i: (i, j)),
        out_shape=jax.ShapeDtypeStruct((m, n), out_dtype),
        compiler_params=_params("parallel", "parallel"),
        name="mm",
    )(a, w)


def _mm2_kernel(a1_ref, a2_ref, w1_ref, w2_ref, o_ref):
    acc = _dot(a1_ref[...].astype(BF16), w1_ref[...])
    acc = acc + _dot(a2_ref[...].astype(BF16), w2_ref[...])
    o_ref[...] = acc


def _mm2(a1, a2, w, tm=1024, tn=1024):
    m, k = a1.shape
    n = w.shape[1]
    tm = min(tm, m)
    tn = min(tn, n)
    return pl.pallas_call(
        _mm2_kernel,
        grid=(n // tn, m // tm),
        in_specs=[pl.BlockSpec((tm, k), lambda j, i: (i, 0)),
                  pl.BlockSpec((tm, k), lambda j, i: (i, 0)),
                  pl.BlockSpec((k, tn), lambda j, i: (0, j)),
                  pl.BlockSpec((k, tn), lambda j, i: (1, j))],
        out_specs=pl.BlockSpec((tm, tn), lambda j, i: (i, j)),
        out_shape=jax.ShapeDtypeStruct((m, n), F32),
        compiler_params=_params("parallel", "parallel"),
        name="mm2",
    )(a1, a2, w, w)


def _fourier_proj_kernel(z_ref, w_ref, a_ref, b_ref):
    r = _dot(z_ref[...], w_ref[0])
    a_ref[...] = r[:, :GROUP_W].astype(a_ref.dtype)
    b_ref[...] = r[:, GROUP_W:].astype(b_ref.dtype)


def _fourier_proj(z, w_ab):
    t = z.shape[0]
    tm = 1024
    wid = N_FOURIER * GROUP_W
    return pl.pallas_call(
        _fourier_proj_kernel,
        grid=(N_FOURIER, t // tm),
        in_specs=[pl.BlockSpec((tm, GROUP_W), lambda g, i: (i, g)),
                  pl.BlockSpec((1, GROUP_W, 2 * GROUP_W), lambda g, i: (g, 0, 0))],
        out_specs=[pl.BlockSpec((tm, GROUP_W), lambda g, i: (i, g)),
                   pl.BlockSpec((tm, GROUP_W), lambda g, i: (i, g))],
        out_shape=[jax.ShapeDtypeStruct((t, wid), BF16), jax.ShapeDtypeStruct((t, wid), BF16)],
        compiler_params=_params("parallel", "parallel"),
        name="fourier_proj",
    )(z, w_ab)


def _fft1_kernel(a_ref, b_ref, c_ref, s_ref, twc_ref, tws_ref, tr_ref, ti_ref):
    a = a_ref[0]
    b = b_ref[0]
    c = c_ref[...]
    s = s_ref[...]
    tr = _dot(c, a) + _dot(s, b)
    ti = _dot(c, b) - _dot(s, a)
    reps = a.shape[1] // LANES
    wc = jnp.concatenate([twc_ref[0]] * reps, axis=1)
    ws = jnp.concatenate([tws_ref[0]] * reps, axis=1)
    tr_ref[0] = (tr * wc + ti * ws).astype(tr_ref.dtype)
    ti_ref[0] = (ti * wc - tr * ws).astype(ti_ref.dtype)


def _fft2_kernel(tr_ref, ti_ref, lr_ref, li_ref, o_ref):
    kb, n2, cols = tr_ref.shape[1:]
    tr = tr_ref[0].reshape(kb * n2, cols)
    ti = ti_ref[0].reshape(kb * n2, cols)
    res = _dot(lr_ref[...], tr) + _dot(li_ref[...], ti)
    o_ref[0] = res.reshape(n2, kb, cols)


def _seq_dft_real(a, b, batch, seq):
    wid = a.shape[1]
    n1 = FFT_N1
    n2 = seq // n1
    kb = FFT_K1_BLOCK
    k = np.arange(n1)
    ang1 = 2.0 * np.pi * np.outer(k, k) / n1
    c1 = jnp.asarray(np.cos(ang1), BF16)
    s1 = jnp.asarray(np.sin(ang1), BF16)
    angt = 2.0 * np.pi * np.outer(np.arange(n2), k) / seq
    twc = jnp.asarray(np.broadcast_to(np.cos(angt)[:, :, None], (n2, n1, LANES)), F32)
    tws = jnp.asarray(np.broadcast_to(np.sin(angt)[:, :, None], (n2, n1, LANES)), F32)
    a3 = a.reshape(batch, n1, n2 * wid)
    b3 = b.reshape(batch, n1, n2 * wid)
    tr, ti = pl.pallas_call(
        _fft1_kernel,
        grid=(batch, n2),
        in_specs=[pl.BlockSpec((1, n1, wid), lambda bb, j: (bb, 0, j)),
                  pl.BlockSpec((1, n1, wid), lambda bb, j: (bb, 0, j)),
                  pl.BlockSpec((n1, n1), lambda bb, j: (0, 0)),
                  pl.BlockSpec((n1, n1), lambda bb, j: (0, 0)),
                  pl.BlockSpec((1, n1, LANES), lambda bb, j: (j, 0, 0)),
                  pl.BlockSpec((1, n1, LANES), lambda bb, j: (j, 0, 0))],
        out_specs=[pl.BlockSpec((1, n1, wid), lambda bb, j: (bb, 0, j)),
                   pl.BlockSpec((1, n1, wid), lambda bb, j: (bb, 0, j))],
        out_shape=[jax.ShapeDtypeStruct((batch, n1, n2 * wid), BF16)] * 2,
        compiler_params=_params("parallel", "parallel"),
        name="fft_stage1",
    )(a3, b3, c1, s1, twc, tws)
    k2 = np.arange(n2)
    ang2 = 2.0 * np.pi * np.outer(k2, k2) / n2
    eye = np.eye(kb)
    lr = np.einsum('kn,jJ->kjJn', np.cos(ang2), eye).reshape(n2 * kb, kb * n2)
    li = np.einsum('kn,jJ->kjJn', np.sin(ang2), eye).reshape(n2 * kb, kb * n2)
    tr4 = tr.reshape(batch, n1, n2, wid)
    ti4 = ti.reshape(batch, n1, n2, wid)
    out = pl.pallas_call(
        _fft2_kernel,
        grid=(batch, n1 // kb),
        in_specs=[pl.BlockSpec((1, kb, n2, wid), lambda bb, j: (bb, j, 0, 0)),
                  pl.BlockSpec((1, kb, n2, wid), lambda bb, j: (bb, j, 0, 0)),
                  pl.BlockSpec((n2 * kb, kb * n2), lambda bb, j: (0, 0)),
                  pl.BlockSpec((n2 * kb, kb * n2), lambda bb, j: (0, 0))],
        out_specs=pl.BlockSpec((1, n2, kb, wid), lambda bb, j: (bb, 0, j, 0)),
        out_shape=jax.ShapeDtypeStruct((batch, n2, n1, wid), F32),
        compiler_params=_params("parallel", "parallel"),
        name="fft_stage2",
    )(tr4, ti4, jnp.asarray(lr, BF16), jnp.asarray(li, BF16))
    return out.reshape(batch * seq, wid)


def _fourier_weights(w_fourier, seq):
    g, cg, _ = w_fourier.shape
    k = np.arange(cg)
    ang = 2.0 * np.pi * np.outer(k, k) / cg
    norm = 1.0 / math.sqrt(seq * cg)
    cs = jnp.asarray(np.concatenate([np.cos(ang), -np.sin(ang)], axis=0) * norm, BF16)
    wf = jnp.transpose(w_fourier, (1, 0, 2)).reshape(cg, g * cg).astype(BF16)
    r = _mm(cs, wf, F32)
    r = r.reshape(2, cg, g, cg)
    return jnp.concatenate([r[0], r[1]], axis=-1).transpose(1, 0, 2).astype(BF16)


POOL_TM = 256
POOL_PAD = 128


def _pool_kernel(z_ref, band_ref, w_ref, ps_ref, o_ref, zp_ref, *, win, seq):
    cg = z_ref.shape[2]
    zp_ref[0:POOL_PAD, :] = jnp.zeros((POOL_PAD, cg), BF16)
    zp_ref[seq + POOL_PAD:seq + 2 * POOL_PAD, :] = jnp.zeros((POOL_PAD, cg), BF16)
    zp_ref[POOL_PAD:seq + POOL_PAD, :] = z_ref[0]
    band = band_ref[...]
    w = w_ref[0]
    ps = ps_ref[0]
    half = win // 2

    def body(i, carry):
        t0 = pl.multiple_of(i * POOL_TM, POOL_TM)
        slab = zp_ref[pl.ds(t0, POOL_TM + 2 * POOL_PAD), :]
        sums = _dot(band, slab)
        t = t0 + lax.broadcasted_iota(jnp.int32, (POOL_TM, cg), 0)
        cnt = (jnp.minimum(t + half, seq) - jnp.maximum(t - half, 0)).astype(F32)
        zc = zp_ref[pl.ds(t0 + POOL_PAD, POOL_TM), :].astype(F32)
        p = sums / cnt - zc
        y = _dot(p.astype(BF16), w) * ps
        o_ref[0, pl.ds(t0, POOL_TM), :] = y.astype(o_ref.dtype)
        return carry

    lax.fori_loop(0, seq // POOL_TM, body, 0)


def _pool_group(z3, w_pool, pool_scale, g, batch, seq):
    win = POOL_WINDOWS[g]
    half = win // 2
    tau = np.arange(POOL_TM)[:, None]
    kap = np.arange(POOL_TM + 2 * POOL_PAD)[None, :]
    off = kap - POOL_PAD - tau
    band = jnp.asarray(((off >= -half) & (off < half)).astype(np.float32), BF16)
    col = N_FOURIER + g
    return pl.pallas_call(
        functools.partial(_pool_kernel, win=win, seq=seq),
        grid=(batch,),
        in_specs=[pl.BlockSpec((1, seq, GROUP_W), lambda bb: (bb, 0, col)),
                  pl.BlockSpec((POOL_TM, POOL_TM + 2 * POOL_PAD), lambda bb: (0, 0)),
                  pl.BlockSpec((1, GROUP_W, GROUP_W), lambda bb: (g, 0, 0)),
                  pl.BlockSpec((1, 1, GROUP_W), lambda bb: (g, 0, 0))],
        out_specs=pl.BlockSpec((1, seq, GROUP_W), lambda bb: (bb, 0, 0)),
        out_shape=jax.ShapeDtypeStruct((batch, seq, GROUP_W), BF16),
        scratch_shapes=[pltpu.VMEM((seq + 2 * POOL_PAD, GROUP_W), BF16)],
        compiler_params=_params("parallel"),
        name=f"pool_w{win}",
    )(z3, band, w_pool, pool_scale)


def _fourier_pool_mixer(h, w_in, w_fourier, w_pool, pool_scale, w_out, batch, seq):
    t = h.shape[0]
    z = _mm(h, w_in.astype(BF16), BF16)
    a, b = _fourier_proj(z, _fourier_weights(w_fourier, seq))
    yf = _seq_dft_real(a, b, batch, seq)
    z3 = z.reshape(batch, seq, z.shape[1])
    wp = w_pool.astype(BF16)
    ps = pool_scale.reshape(N_POOL, 1, GROUP_W)
    yp = jnp.concatenate([_pool_group(z3, wp, ps, g, batch, seq) for g in range(N_POOL)], axis=-1)
    return _mm2(yf, yp.reshape(t, N_POOL * GROUP_W), w_out.astype(BF16))


def _qk_prep_kernel(x_ref, g_ref, cos_ref, sin_ref, o_ref):
    x = x_ref[...].astype(F32)
    y = x * lax.rsqrt(jnp.mean(x * x, axis=-1, keepdims=True) + LN_EPS) * g_ref[0]
    lane = lax.broadcasted_iota(jnp.int32, y.shape, 1)
    first = (lane % (HEAD // 2)) < (HEAD // 4)
    partner = jnp.where(first, pltpu.roll(y, HEAD - HEAD // 4, 1), pltpu.roll(y, HEAD // 4, 1))
    o_ref[...] = (y * cos_ref[...] + partner * sin_ref[...]).astype(o_ref.dtype)


def _rope_tables(seq):
    quarter = HEAD // 4
    inv = ROPE_THETA ** (-np.arange(quarter, dtype=np.float64) / quarter)
    t = np.arange(seq)
    ang_r = (t // GRID_COLS)[:, None] * inv[None, :]
    ang_c = (t % GRID_COLS)[:, None] * inv[None, :]
    cos = np.concatenate([np.cos(ang_r), np.cos(ang_r), np.cos(ang_c), np.cos(ang_c)], axis=1)
    sin = np.concatenate([-np.sin(ang_r), np.sin(ang_r), -np.sin(ang_c), np.sin(ang_c)], axis=1)
    return jnp.asarray(cos, F32), jnp.asarray(sin, F32)


def _qk_prep(proj, gains, seq):
    t = proj.shape[0]
    nh = gains.shape[0]
    tm = 1024
    per_b = seq // tm
    cos, sin = _rope_tables(seq)
    return pl.pallas_call(
        _qk_prep_kernel,
        grid=(t // tm, nh),
        in_specs=[pl.BlockSpec((tm, HEAD), lambda i, hh: (i, hh)),
                  pl.BlockSpec((1, 1, HEAD), lambda i, hh: (hh, 0, 0)),
                  pl.BlockSpec((tm, HEAD), lambda i, hh: (i % per_b, 0)),
                  pl.BlockSpec((tm, HEAD), lambda i, hh: (i % per_b, 0))],
        out_specs=pl.BlockSpec((tm, HEAD), lambda i, hh: (i, hh)),
        out_shape=jax.ShapeDtypeStruct((t, nh * HEAD), BF16),
        compiler_params=_params("parallel", "parallel"),
        name="qk_prep",
    )(proj, gains, cos, sin)


FLASH_TQ = 512
FLASH_TK = 1024
FLASH_UNROLL = 4


def _flash_kernel(q_ref, k_ref, v_ref, o_ref):
    seq = k_ref.shape[0]
    tq = q_ref.shape[0]
    for hh in range(N_Q_HEADS // N_KV_HEADS):
        q = q_ref[:, hh * HEAD:(hh + 1) * HEAD]

        def body(c, carry):
            m, l, acc = carry
            c0 = pl.multiple_of(c * FLASH_TK, FLASH_TK)
            k = k_ref[pl.ds(c0, FLASH_TK), :]
            v = v_ref[pl.ds(c0, FLASH_TK), :]
            s = _dot_nt(q, k)
            m_new = jnp.maximum(m, jnp.max(s, axis=-1, keepdims=True))
            p = jnp.exp2(s - m_new)
            corr = jnp.exp2(m - m_new)
            l = l * corr + jnp.sum(p, axis=-1, keepdims=True)
            acc = acc * corr + _dot(p.astype(BF16), v)
            return m_new, l, acc

        init = (jnp.full((tq, 1), NEG_BIG, F32), jnp.zeros((tq, 1), F32), jnp.zeros((tq, HEAD), F32))
        m, l, acc = lax.fori_loop(0, seq // FLASH_TK, body, init, unroll=FLASH_UNROLL)
        o_ref[:, hh * HEAD:(hh + 1) * HEAD] = (acc / l).astype(o_ref.dtype)


def _gqa_attention(qk, proj, batch, seq):
    t = qk.shape[0]
    per_b = seq // FLASH_TQ
    gw = (N_Q_HEADS // N_KV_HEADS) * HEAD
    k_col = N_Q_HEADS
    v_col = N_Q_HEADS + N_KV_HEADS
    return pl.pallas_call(
        _flash_kernel,
        grid=(batch, N_KV_HEADS, per_b),
        in_specs=[pl.BlockSpec((FLASH_TQ, gw), lambda bb, kv, i: (bb * per_b + i, kv)),
                  pl.BlockSpec((seq, HEAD), lambda bb, kv, i: (bb, k_col + kv)),
                  pl.BlockSpec((seq, HEAD), lambda bb, kv, i: (bb, v_col + kv))],
        out_specs=pl.BlockSpec((FLASH_TQ, gw), lambda bb, kv, i: (bb * per_b + i, kv)),
        out_shape=jax.ShapeDtypeStruct((t, N_Q_HEADS * HEAD), BF16),
        compiler_params=_params("parallel", "parallel", "parallel"),
        name="gqa_flash",
    )(qk, qk, proj)


NA_ROW_BLOCK = 16


def _na_kernel(q_ref, k_ref, v_ref, bias_ref, o_ref, *, rows, scale):
    i = pl.program_id(2)
    span = NA_ROWS * GRID_COLS
    starts, scores = [], []
    for rr in range(NA_ROW_BLOCK):
        r = i * NA_ROW_BLOCK + rr
        rs = jnp.clip(r - NA_ROWS // 2, 0, rows - NA_ROWS)
        dr0 = rs - r + (NA_ROWS - 1)
        k0 = pl.multiple_of(rs * GRID_COLS, GRID_COLS)
        q = q_ref[rr * GRID_COLS:(rr + 1) * GRID_COLS, :]
        starts.append(k0)
        scores.append(_dot_nt(q, k_ref[pl.ds(k0, span), :]) * scale + bias_ref[0, dr0])
    probs = []
    for s in scores:
        p = jnp.exp(s - jnp.max(s, axis=-1, keepdims=True))
        probs.append((p / jnp.sum(p, axis=-1, keepdims=True)).astype(BF16))
    outs = [_dot(p, v_ref[pl.ds(k0, span), :]) for p, k0 in zip(probs, starts)]
    o_ref[...] = jnp.concatenate(outs, axis=0).astype(o_ref.dtype)


def _na_bias_table(rpb):
    cols = np.arange(GRID_COLS)
    start = np.clip(cols - NA_COLS // 2, 0, GRID_COLS - NA_COLS)
    kc = np.arange(GRID_COLS)
    valid = (kc[None, :] >= start[:, None]) & (kc[None, :] < start[:, None] + NA_COLS)
    off = np.clip(kc[None, :] - cols[:, None] + (NA_COLS - 1), 0, 2 * NA_COLS - 2)
    tab = rpb[:, :, off]
    tab = jnp.where(jnp.asarray(valid)[None, None], tab, NEG_BIG)
    per_dr0 = [jnp.concatenate([tab[:, d + j] for j in range(NA_ROWS)], axis=-1) for d in range(NA_ROWS)]
    return jnp.stack(per_dr0, axis=1).astype(F32)


def _neighbourhood_attention(proj, rpb, batch, seq):
    t = proj.shape[0]
    rows = seq // GRID_COLS
    qb = NA_ROW_BLOCK * GRID_COLS
    per_b = seq // qb
    q_col = (N_Q_HEADS + 2 * N_KV_HEADS)
    k_col = q_col + N_NA_HEADS
    v_col = k_col + N_NA_HEADS
    bias = _na_bias_table(rpb)
    return pl.pallas_call(
        functools.partial(_na_kernel, rows=rows, scale=HEAD ** -0.5),
        grid=(batch, N_NA_HEADS, per_b),
        in_specs=[pl.BlockSpec((qb, HEAD), lambda bb, hh, i: (bb * per_b + i, q_col + hh)),
                  pl.BlockSpec((seq, HEAD), lambda bb, hh, i: (bb, k_col + hh)),
                  pl.BlockSpec((seq, HEAD), lambda bb, hh, i: (bb, v_col + hh)),
                  pl.BlockSpec((1, NA_ROWS, GRID_COLS, NA_ROWS * GRID_COLS), lambda bb, hh, i: (hh, 0, 0, 0))],
        out_specs=pl.BlockSpec((qb, HEAD), lambda bb, hh, i: (bb * per_b + i, hh)),
        out_shape=jax.ShapeDtypeStruct((t, N_NA_HEADS * HEAD), BF16),
        compiler_params=_params("parallel", "parallel", "parallel"),
        name="natten",
    )(proj, proj, proj, bias)


def _attention_mixer(h, w_in, q_norm, k_norm, rpb, w_out, batch, seq):
    proj = _mm(h, w_in.astype(BF16), BF16, tn=1152)
    q_gain = HEAD ** -0.5 * math.log2(math.e)
    gains = jnp.concatenate([jnp.tile(q_norm[None] * q_gain, (N_Q_HEADS, 1)),
                             jnp.tile(k_norm[None], (N_KV_HEADS, 1))], axis=0)
    qk = _qk_prep(proj, gains.reshape(N_Q_HEADS + N_KV_HEADS, 1, HEAD).astype(F32), seq)
    yc = _gqa_attention(qk, proj, batch, seq)
    yd = _neighbourhood_attention(proj, rpb, batch, seq)
    return _mm2(yc, yd, w_out.astype(BF16))


def _row(v, i, fill):
    rows = lax.broadcasted_iota(jnp.int32, v.shape, 0)
    return jnp.max(jnp.where(rows == i, v, fill), axis=0, keepdims=True)


def _top16(s, ids, big):
    rows, n = s.shape
    quarter = rows // 4
    lst = [s[i * quarter:(i + 1) * quarter] for i in range(4)]
    lid = list(ids)

    def order(i, j):
        a, b, ia, ib = lst[i], lst[j], lid[i], lid[j]
        swap = jnp.where(b > a, 1.0, jnp.where(b < a, 0.0, jnp.where(ib < ia, 1.0, 0.0))) > 0.5
        lst[i], lst[j] = jnp.where(swap, b, a), jnp.where(swap, a, b)
        lid[i], lid[j] = jnp.where(swap, ib, ia), jnp.where(swap, ia, ib)

    for i, j in ((0, 1), (2, 3), (0, 2), (1, 3), (1, 2)):
        order(i, j)
    slot = lax.broadcasted_iota(jnp.int32, (PEER_TOPK, n), 0)
    vals = jnp.zeros((PEER_TOPK, n), F32)
    idxs = jnp.zeros((PEER_TOPK, n), F32)
    for it in range(PEER_TOPK):
        m = jnp.max(lst[0], axis=0, keepdims=True)
        ix = jnp.min(jnp.where(lst[0] == m, lid[0], big), axis=0, keepdims=True)
        hit = lid[0] == ix
        for k in range(3):
            lst[k] = jnp.where(hit, lst[k + 1], lst[k])
            lid[k] = jnp.where(hit, lid[k + 1], lid[k])
        lst[3] = jnp.where(hit, -jnp.inf, lst[3])
        vals = jnp.where(slot == it, m, vals)
        idxs = jnp.where(slot == it, ix, idxs)
    return vals, idxs


def _pick(table, sel):
    out = jnp.zeros_like(table)
    for r in range(PEER_TOPK):
        out = jnp.where(sel == r, _row(table, r, -1.0), out)
    return out


def _peer_topk_kernel(q_ref, keys_ref, e1_ref, e2_ref, g_ref):
    tm = q_ref.shape[0]
    half = q_ref.shape[1] // 2
    k = PEER_TOPK
    s1 = _dot_nt(keys_ref[0, 0], q_ref[:, :half])
    s2 = _dot_nt(keys_ref[0, 1], q_ref[:, half:])
    key_lo = lax.broadcasted_iota(jnp.int32, (PEER_KEYS // 4, tm), 0).astype(F32)
    key_ids = tuple(key_lo + float(i * (PEER_KEYS // 4)) for i in range(4))
    v1, i1 = _top16(s1, key_ids, float(PEER_KEYS))
    v2, i2 = _top16(s2, key_ids, float(PEER_KEYS))
    i8 = lax.broadcasted_iota(jnp.int32, (8, tm), 0)
    i16 = lax.broadcasted_iota(jnp.int32, (k, tm), 0)
    ninf = -jnp.inf
    cand = [_row(v1, 0, ninf) + v2]
    ids = [i16]
    for i in (1, 2, 3):
        cand.append(_row(v1, i, ninf) + v2[0:8])
        ids.append(i * k + i8)
    cand.append(v1[8:16] + _row(v2, 0, ninf))
    ids.append((i8 + 8) * k)
    for j in (0, 1):
        cand.append(jnp.where(i8 < 4, ninf, v1[0:8] + _row(v2, j, ninf)))
        ids.append(i8 * k + j)
    cand = jnp.concatenate(cand, axis=0)
    ids = (ids[0].astype(F32),) + tuple(jnp.concatenate(ids[i:i + 2], axis=0).astype(F32) for i in (1, 3, 5))
    sc, cf = _top16(cand, ids, float(k * k))
    ci = cf.astype(jnp.int32)
    ex = jnp.exp(sc - _row(sc, 0, ninf))
    e1_ref[0] = _pick(i1, lax.shift_right_logical(ci, 4))
    e2_ref[0] = _pick(i2, lax.bitwise_and(ci, k - 1))
    g_ref[0] = ex / jnp.sum(ex, axis=0, keepdims=True)


def _peer_topk(q, sub_keys):
    t = q.shape[0]
    tm = 512
    qd = q.shape[1] // PEER_HEADS
    out = jax.ShapeDtypeStruct((PEER_HEADS, PEER_TOPK, t), F32)
    spec = pl.BlockSpec((1, PEER_TOPK, tm), lambda i, hh: (hh, 0, i))
    return pl.pallas_call(
        _peer_topk_kernel,
        grid=(t // tm, PEER_HEADS),
        in_specs=[pl.BlockSpec((tm, qd), lambda i, hh: (i, hh)),
                  pl.BlockSpec((1, 2, PEER_KEYS, qd // 2), lambda i, hh: (hh, 0, 0, 0))],
        out_specs=[spec, spec, spec],
        out_shape=[out, out, out],
        compiler_params=_params("parallel", "parallel"),
        name="peer_topk",
    )(q, sub_keys)


PEER_TM = 512
PEER_SUB = 512
PEER_BUILD_UNROLL = 32
PEER_ROW_PAD = 8
HI16 = 0xFFFF0000


def _peer_experts_kernel(h_ref, e1_ref, e2_ref, gt_ref, ulo_ref, uhi_ref, vlo_ref, vhi_ref, o_ref,
                         gmat, e1s, e2s, gs):
    c = pl.program_id(1)
    tm = h_ref.shape[0]
    slots = PEER_HEADS * PEER_TOPK
    pitch = gmat.shape[0] // tm
    half = pitch - PEER_ROW_PAD

    @pl.when(c == 0)
    def _():
        o_ref[...] = jnp.zeros_like(o_ref)
        e1s[...] = e1_ref[...].reshape(slots, tm).T
        e2s[...] = e2_ref[...].reshape(slots, tm).T
        gs[...] = gt_ref[...].reshape(slots, tm).T
        key = lax.broadcasted_iota(jnp.int32, (PEER_KEYS, slots), 0).astype(F32)

        def body(t, carry):
            r1 = e1s[pl.ds(t, 1), :]
            r2 = e2s[pl.ds(t, 1), :]
            rg = gs[pl.ds(t, 1), :]
            a = jnp.where(key == r1, 1.0, 0.0).astype(BF16)
            b = jnp.where(key == r2, rg, 0.0).astype(BF16)
            g = _dot_nt(a, b).astype(BF16).astype(F32)
            hi = pltpu.bitcast(g[:half], jnp.uint32)
            lo = pltpu.bitcast(g[half:], jnp.uint32)
            gmat[pl.ds(pl.multiple_of(t * pitch, PEER_ROW_PAD), half), :] = hi | (lo >> 16)
            return carry

        lax.fori_loop(0, tm, body, 0, unroll=PEER_BUILD_UNROLL)

    per_sub = ulo_ref.shape[0] // PEER_KEYS
    words = [gmat[pl.ds(c * per_sub + j, tm, stride=pitch), :] for j in range(per_sub)]
    g_lo = jnp.concatenate([pltpu.bitcast(w & jnp.uint32(HI16), F32) for w in words], axis=1)
    g_hi = jnp.concatenate([pltpu.bitcast(w << 16, F32) for w in words], axis=1)
    for u_ref, v_ref, g in ((ulo_ref, vlo_ref, g_lo), (uhi_ref, vhi_ref, g_hi)):
        a = _dot_nt(h_ref[...], u_ref[...])
        act = 0.5 * a * (1.0 + lax.erf(a * (1.0 / math.sqrt(2.0))))
        o_ref[...] += _dot((act * g).astype(BF16), v_ref[...])


def _peer_experts(h, e1, e2, gates, u, v, layer):
    t, d = h.shape
    n_exp = v.shape[1]
    tm = min(PEER_TM, t)
    half = PEER_KEYS // 2
    steps = n_exp // 2 // PEER_SUB
    slots = PEER_HEADS * PEER_TOPK
    sel = pl.BlockSpec((PEER_HEADS, PEER_TOPK, tm), lambda i, c: (0, 0, i))
    lo_blk = pl.BlockSpec((None, PEER_SUB, d), lambda i, c: (layer, c, 0))
    hi_blk = pl.BlockSpec((None, PEER_SUB, d), lambda i, c: (layer, steps + c, 0))
    return pl.pallas_call(
        _peer_experts_kernel,
        grid=(t // tm, steps),
        in_specs=[pl.BlockSpec((tm, d), lambda i, c: (i, 0)), sel, sel, sel, lo_blk, hi_blk, lo_blk, hi_blk],
        out_specs=pl.BlockSpec((tm, d), lambda i, c: (i, 0)),
        out_shape=jax.ShapeDtypeStruct((t, d), F32),
        scratch_shapes=[pltpu.VMEM((tm * (half + PEER_ROW_PAD), PEER_KEYS), jnp.uint32)]
        + [pltpu.VMEM((tm, slots), F32)] * 3,
        compiler_params=_params("parallel", "arbitrary"),
        name="peer_experts",
    )(h, e1, e2, gates, u, u, v, v)


def _peer(h, w_q, sub_keys, u_all, v_all, layer):
    q = _mm(h, w_q.astype(BF16), BF16)
    e1, e2, gates = _peer_topk(q, sub_keys.astype(BF16))
    return _peer_experts(h, e1, e2, gates, u_all, v_all, layer)


def kernel(x, c, ada_w, ada_b, ln_g, ln_b, fp_w_in, fp_w_fourier, fp_w_pool, fp_pool_scale, fp_w_out,
           at_w_in, at_q_norm, at_k_norm, at_rpb, at_w_out, peer_w_q, peer_sub_keys, peer_u, peer_v):
    batch, seq, d = x.shape
    depth = ada_w.shape[0]
    alpha = (2.0 * depth) ** 0.25
    xt = x.reshape(batch * seq, d)
    ada = _adaln_all(c, ada_w, ada_b)[:, :batch]
    shift = ada[:, :, None, :d]
    scale = ada[:, :, None, d:2 * d]
    gate = ada[:, :, None, 2 * d:]
    u_all = peer_u.astype(BF16)
    v_all = peer_v.astype(BF16)
    h = _modulate(xt, shift[0], scale[0], seq)
    for n in range(2 * depth):
        l, sub = divmod(n, 2)
        i = l // 2
        if sub == 1:
            y = _peer(h, peer_w_q[l], peer_sub_keys[l], u_all, v_all, l)
        elif l % 2 == 0:
            y = _fourier_pool_mixer(h, fp_w_in[i], fp_w_fourier[i], fp_w_pool[i], fp_pool_scale[i],
                                    fp_w_out[i], batch, seq)
        else:
            y = _attention_mixer(h, at_w_in[i], at_q_norm[i], at_k_norm[i], at_rpb[i], at_w_out[i],
                                 batch, seq)
        if n + 1 < 2 * depth:
            xt, h = _postnorm_modulate(xt, y, gate[n], ln_g[l, sub], ln_b[l, sub],
                                       shift[n + 1], scale[n + 1], seq, alpha)
        else:
            xt = _postnorm(xt, y, gate[n], ln_g[l, sub], ln_b[l, sub], seq, alpha)
    return xt.reshape(batch, seq, d)
```
